```python
import jax
import jax.numpy as jnp
from jax import lax
import numpy as np


D_MODEL = 2048
BATCH = 2
SEQ = 16384
DEPTH = 4
DEC_BATCH = 16
DEC_SEQ = 2048
PAST_LEN = 128

CONV_DIM = D_MODEL
CONV_WIDTH = 31
GLA_HEADS = 4
GLA_DK = D_MODEL // (2 * GLA_HEADS)
GLA_DV = D_MODEL // GLA_HEADS
GLA_KDIM = GLA_HEADS * GLA_DK
GLA_VDIM = GLA_HEADS * GLA_DV
GATE_RANK = 16
GATE_TAU = 16.0
CHUNK = 64
FFN_DIM = 5632
FFN_CONV = 3
PLE_DIM = 256
EPS = 1e-6
IN_SIZES = (CONV_DIM, CONV_DIM, GLA_KDIM, GLA_KDIM, GLA_VDIM, GLA_VDIM, GATE_RANK, GATE_RANK, D_MODEL, D_MODEL)
N_IN = 2 * (CONV_DIM + GLA_KDIM + GLA_VDIM + GATE_RANK + D_MODEL)

kernel_name = 'bidir_conv_gla_hybrid_encoder'


def rmsnorm(x, g):
    xf = x.astype(jnp.float32)
    y = xf * lax.rsqrt(jnp.mean(xf * xf, axis=-1, keepdims=True) + EPS)
    return (y * g.astype(jnp.float32)).astype(x.dtype)


def layernorm(x, g, b):
    xf = x.astype(jnp.float32)
    xc = xf - jnp.mean(xf, axis=-1, keepdims=True)
    var = jnp.mean(xc * xc, axis=-1, keepdims=True)
    return (xc * lax.rsqrt(var + EPS) * g.astype(jnp.float32) + b.astype(jnp.float32)).astype(x.dtype)


def dwconv(x, w, b):
    width, c = w.shape
    pad = width // 2
    y = lax.conv_general_dilated(x, w[:, None, :].astype(x.dtype), window_strides=(1,),
                                 padding=[(pad, pad)], dimension_numbers=('NWC', 'WIO', 'NWC'),
                                 feature_group_count=c)
    return y + b.astype(x.dtype)


def gla_scan(q, k, v, la):
    bsz, s, h, dk = q.shape
    dv = v.shape[-1]
    n = s // CHUNK

    def to_chunks(t):
        return t.astype(jnp.float32).reshape(bsz, n, CHUNK, h, t.shape[-1]).transpose(1, 0, 3, 2, 4)

    qc, kc, vc, lac = to_chunks(q), to_chunks(k), to_chunks(v), to_chunks(la)
    mask = jnp.tril(jnp.ones((CHUNK, CHUNK), dtype=bool))[:, :, None]

    def step(state, inp):
        qi, ki, vi, lai = inp
        b = jnp.cumsum(lai, axis=2)
        b_last = b[:, :, -1:, :]
        diff = b[:, :, :, None, :] - b[:, :, None, :, :]
        decay = jnp.exp(jnp.where(mask, diff, -jnp.inf))
        scores = jnp.einsum('bhid,bhjd,bhijd->bhij', qi, ki, decay)
        o_intra = jnp.einsum('bhij,bhjv->bhiv', scores, vi)
        o_inter = jnp.einsum('bhid,bhdv->bhiv', qi * jnp.exp(b), state)
        k_dec = ki * jnp.exp(b_last - b)
        new_state = jnp.exp(b_last)[:, :, 0, :, None] * state + jnp.einsum('bhjd,bhjv->bhdv', k_dec, vi)
        return new_state, o_intra + o_inter

    state0 = jnp.zeros((bsz, h, dk, dv), jnp.float32)
    _, o = lax.scan(step, state0, (qc, kc, vc, lac))
    return o.transpose(1, 0, 3, 2, 4).reshape(bsz, s, h, dv).astype(v.dtype)


def gla_bidir(q, k, v, la_f, la_b):
    fwd = gla_scan(q, k, v, la_f)
    flip = lambda t: jnp.flip(t, axis=1)
    bwd = flip(gla_scan(flip(q), flip(k), flip(v), flip(la_b)))
    return fwd + bwd


def encoder_layer(x, p, g_mix, w_in, b_in, w_conv_dw, b_conv_dw, g_conv_ln, b_conv_ln, w_conv_out,
                  w_alpha_f, b_alpha_f, w_alpha_b, b_alpha_b, g_gla, w_gla_out, w_out,
                  g_ffn, w_ffn_up, w_ffn_dw, b_ffn_dw, w_ffn_down, g_ple, w_ple_gate, w_ple_proj):
    bsz, s, _ = x.shape
    h = rmsnorm(x, g_mix)
    proj = h @ w_in + b_in
    splits = np.cumsum(IN_SIZES)[:-1].tolist()
    c_val, c_gate, q, k, v, r, z_f, z_b, s_conv, s_gla = jnp.split(proj, splits, axis=-1)
    u = c_val * jax.nn.sigmoid(c_gate)
    u = dwconv(u, w_conv_dw, b_conv_dw)
    u = jax.nn.silu(layernorm(u, g_conv_ln, b_conv_ln))
    y_conv = u @ w_conv_out
    q = q.reshape(bsz, s, GLA_HEADS, GLA_DK) * (GLA_DK ** -0.5)
    k = k.reshape(bsz, s, GLA_HEADS, GLA_DK)
    v = v.reshape(bsz, s, GLA_HEADS, GLA_DV)
    la_f = jax.nn.log_sigmoid((z_f @ w_alpha_f + b_alpha_f).astype(jnp.float32)) / GATE_TAU
    la_b = jax.nn.log_sigmoid((z_b @ w_alpha_b + b_alpha_b).astype(jnp.float32)) / GATE_TAU
    la_f = la_f.reshape(bsz, s, GLA_HEADS, GLA_DK)
    la_b = la_b.reshape(bsz, s, GLA_HEADS, GLA_DK)
    o = gla_bidir(q, k, v, la_f, la_b)
    o = rmsnorm(o, g_gla.reshape(GLA_HEADS, GLA_DV))
    o = o.reshape(bsz, s, GLA_VDIM) * jax.nn.silu(r)
    y_gla = o @ w_gla_out
    mixed = jax.nn.sigmoid(s_conv) * y_conv + jax.nn.sigmoid(s_gla) * y_gla
    x = x + mixed @ w_out
    h = rmsnorm(x, g_ffn)
    gt, val = jnp.split(h @ w_ffn_up, 2, axis=-1)
    gt = dwconv(gt, w_ffn_dw, b_ffn_dw)
    x = x + (jax.nn.gelu(gt) * val) @ w_ffn_down
    gate = jax.nn.sigmoid(rmsnorm(x, g_ple) @ w_ple_gate)
    x = x + gate * (p @ w_ple_proj)
    return x


def trunk(x, p, g_mix, w_in, b_in, w_conv_dw, b_conv_dw, g_conv_ln, b_conv_ln, w_conv_out,
          w_alpha_f, b_alpha_f, w_alpha_b, b_alpha_b, g_gla, w_gla_out, w_out,
          g_ffn, w_ffn_up, w_ffn_dw, b_ffn_dw, w_ffn_down, g_ple, w_ple_gate, w_ple_proj, g_final):
    for i in range(DEPTH):
        x = encoder_layer(x, p[i], g_mix[i], w_in[i], b_in[i], w_conv_dw[i], b_conv_dw[i],
                          g_conv_ln[i], b_conv_ln[i], w_conv_out[i], w_alpha_f[i], b_alpha_f[i],
                          w_alpha_b[i], b_alpha_b[i], g_gla[i], w_gla_out[i], w_out[i],
                          g_ffn[i], w_ffn_up[i], w_ffn_dw[i], b_ffn_dw[i], w_ffn_down[i],
                          g_ple[i], w_ple_gate[i], w_ple_proj[i])
    return rmsnorm(x, g_final)


def setup_inputs(seed: int = 0) -> dict:
    key = jax.random.key(seed)
    ks = jax.random.split(key, 32)
    f32 = jnp.float32

    def nrm(i, shape, scale):
        return jax.random.normal(ks[i], shape, f32) * scale

    def gain(i, shape):
        return 1.0 + nrm(i, shape, 0.02)

    L = DEPTH
    return {
        'x_prompt': nrm(0, (BATCH, SEQ, D_MODEL), 1.0),
        'x_sample': nrm(1, (DEC_BATCH, DEC_SEQ, D_MODEL), 1.0),
        'p_prompt': nrm(2, (DEPTH, BATCH, SEQ, PLE_DIM), 1.0),
        'p_sample': nrm(3, (DEPTH, DEC_BATCH, DEC_SEQ, PLE_DIM), 1.0),
        'g_mix': gain(4, (L, D_MODEL)),
        'w_in': nrm(5, (L, D_MODEL, N_IN), D_MODEL ** -0.5),
        'b_in': nrm(6, (L, N_IN), 0.02),
        'w_conv_dw': nrm(7, (L, CONV_WIDTH, CONV_DIM), CONV_WIDTH ** -0.5),
        'b_conv_dw': nrm(8, (L, CONV_DIM), 0.02),
        'g_conv_ln': gain(9, (L, CONV_DIM)),
        'b_conv_ln': nrm(10, (L, CONV_DIM), 0.02),
        'w_conv_out': nrm(11, (L, CONV_DIM, D_MODEL), CONV_DIM ** -0.5),
        'w_alpha_f': nrm(12, (L, GATE_RANK, GLA_KDIM), GATE_RANK ** -0.5),
        'b_alpha_f': 2.0 + nrm(13, (L, GLA_KDIM), 0.5),
        'w_alpha_b': nrm(14, (L, GATE_RANK, GLA_KDIM), GATE_RANK ** -0.5),
        'b_alpha_b': 2.0 + nrm(15, (L, GLA_KDIM), 0.5),
        'g_gla': gain(16, (L, GLA_VDIM)),
        'w_gla_out': nrm(17, (L, GLA_VDIM, D_MODEL), GLA_VDIM ** -0.5),
        'w_out': nrm(18, (L, D_MODEL, D_MODEL), D_MODEL ** -0.5),
        'g_ffn': gain(19, (L, D_MODEL)),
        'w_ffn_up': nrm(20, (L, D_MODEL, 2 * FFN_DIM), D_MODEL ** -0.5),
        'w_ffn_dw': nrm(21, (L, FFN_CONV, FFN_DIM), FFN_CONV ** -0.5),
        'b_ffn_dw': nrm(22, (L, FFN_DIM), 0.02),
        'w_ffn_down': nrm(23, (L, FFN_DIM, D_MODEL), FFN_DIM ** -0.5),
        'g_ple': gain(24, (L, D_MODEL)),
        'w_ple_gate': nrm(25, (L, D_MODEL, D_MODEL), D_MODEL ** -0.5),
        'w_ple_proj': nrm(26, (L, PLE_DIM, D_MODEL), PLE_DIM ** -0.5),
        'g_final': gain(27, (D_MODEL,)),
    }


def reference(x_prompt, x_sample, p_prompt, p_sample, g_mix, w_in, b_in, w_conv_dw, b_conv_dw,
              g_conv_ln, b_conv_ln, w_conv_out, w_alpha_f, b_alpha_f, w_alpha_b, b_alpha_b,
              g_gla, w_gla_out, w_out, g_ffn, w_ffn_up, w_ffn_dw, b_ffn_dw, w_ffn_down,
              g_ple, w_ple_gate, w_ple_proj, g_final):
    y_prompt = trunk(x_prompt, p_prompt, g_mix, w_in, b_in, w_conv_dw, b_conv_dw, g_conv_ln,
                     b_conv_ln, w_conv_out, w_alpha_f, b_alpha_f, w_alpha_b, b_alpha_b, g_gla,
                     w_gla_out, w_out, g_ffn, w_ffn_up, w_ffn_dw, b_ffn_dw, w_ffn_down, g_ple,
                     w_ple_gate, w_ple_proj, g_final)
    y_sample = trunk(x_sample, p_sample, g_mix, w_in, b_in, w_conv_dw, b_conv_dw, g_conv_ln,
                     b_conv_ln, w_conv_out, w_alpha_f, b_alpha_f, w_alpha_b, b_alpha_b, g_gla,
                     w_gla_out, w_out, g_ffn, w_ffn_up, w_ffn_dw, b_ffn_dw, w_ffn_down, g_ple,
                     w_ple_gate, w_ple_proj, g_final)
    return (y_prompt, y_sample)
```

```python
import functools

import jax
import jax.numpy as jnp
from jax import lax
from jax.experimental import pallas as pl
from jax.experimental.pallas import tpu as pltpu

F32 = jnp.float32
BF16 = jnp.bfloat16

D_MODEL = 2048
CONV_WIDTH = 31
GLA_HEADS = 4
GLA_DK = 256
GLA_DV = 512
GLA_KDIM = GLA_HEADS * GLA_DK
GATE_RANK = 16
GATE_TAU = 16.0
FFN_DIM = 5632
PLE_DIM = 256
EPS = 1e-6

LANES = 128
HALO = 16
MIB = 1024 * 1024

N_MAIN = 7 * D_MODEL
COL_CVAL, COL_CGATE, COL_Q, COL_K, COL_V, COL_R, COL_SCONV, COL_SGLA = (
    0, 2048, 4096, 5120, 6144, 8192, 10240, 12288)

GLA_CHUNK = 128
GLA_ROWS = 512


def _params(sem, vmem_mib):
    return pltpu.CompilerParams(dimension_semantics=sem, vmem_limit_bytes=vmem_mib * MIB)


def _tile(total, pref):
    t = min(total, pref)
    while total % t:
        t //= 2
    return t


def _rms(x, g):
    ms = jnp.mean(x * x, axis=-1, keepdims=True)
    return x * lax.rsqrt(ms + EPS) * g


def _const_spec(shape):
    return pl.BlockSpec(shape, lambda *_: (0,) * len(shape), pipeline_mode=pl.Buffered(1))


def _inproj_kernel(x_ref, g_ref, w_ref, b_ref, wz_ref, bz_ref, o_ref, z_ref, h_ref):
    @pl.when(pl.program_id(1) == 0)
    def _():
        hb = _rms(x_ref[...], g_ref[...]).astype(BF16)
        h_ref[...] = hb
        z = jnp.dot(hb, wz_ref[...], preferred_element_type=F32) + bz_ref[...]
        hi = z.astype(BF16)
        lo = (z - hi.astype(F32)).astype(BF16)
        lane = lax.broadcasted_iota(jnp.int32, z.shape, 1) % (3 * GATE_RANK)
        is_lo = (lane >= GATE_RANK) & (lane < 2 * GATE_RANK)
        z_ref[...] = jnp.where(is_lo, lo, hi)

    acc = jnp.dot(h_ref[...], w_ref[...], preferred_element_type=F32) + b_ref[...]
    o_ref[...] = acc.astype(o_ref.dtype)


def _inproj(x, g, w, b, wz, bz):
    t = x.shape[0]
    tm = _tile(t, 1024)
    tn = 1024
    return pl.pallas_call(
        _inproj_kernel,
        grid=(t // tm, N_MAIN // tn),
        in_specs=[
            pl.BlockSpec((tm, D_MODEL), lambda i, j: (i, 0)),
            pl.BlockSpec((1, D_MODEL), lambda i, j: (0, 0)),
            pl.BlockSpec((D_MODEL, tn), lambda i, j: (0, j)),
            pl.BlockSpec((1, tn), lambda i, j: (0, j)),
            pl.BlockSpec((D_MODEL, LANES), lambda i, j: (0, 0)),
            pl.BlockSpec((1, LANES), lambda i, j: (0, 0)),
        ],
        out_specs=[
            pl.BlockSpec((tm, tn), lambda i, j: (i, j)),
            pl.BlockSpec((tm, LANES), lambda i, j: (i, 0)),
        ],
        out_shape=[
            jax.ShapeDtypeStruct((t, N_MAIN), BF16),
            jax.ShapeDtypeStruct((t, LANES), BF16),
        ],
        scratch_shapes=[pltpu.VMEM((tm, D_MODEL), BF16)],
        compiler_params=_params(("parallel", "arbitrary"), 48),
        name="inproj",
    )(x, g, w, b, wz, bz)


def _conv_kernel(cv_ref, cvl_ref, cvr_ref, cg_ref, cgl_ref, cgr_ref, wdw_ref, bdw_ref,
                 gln_ref, bln_ref, wout_ref, y_ref, ext_ref, acc_ref, *, tiles_per_seq, tm):
    t = pl.program_id(0) % tiles_per_seq

    def glu(v_ref, gate_ref):
        return v_ref[...].astype(F32) * jax.nn.sigmoid(gate_ref[...].astype(F32))

    ext_ref[0:HALO, :] = jnp.where(t != 0, glu(cvl_ref, cgl_ref), 0.0)
    ext_ref[HALO:HALO + tm, :] = glu(cv_ref, cg_ref)
    ext_ref[HALO + tm:2 * HALO + tm, :] = jnp.where(t != tiles_per_seq - 1, glu(cvr_ref, cgr_ref), 0.0)

    rows = min(tm, 128)

    def cblock(c, carry):
        c0 = pl.multiple_of(c * LANES, LANES)
        bias = bdw_ref[:, pl.ds(c0, LANES)]
        for r0 in range(0, tm, rows):
            acc = jnp.zeros((rows, LANES), F32) + bias
            for phase in range(8):
                shifts = [s for s in range(phase, CONV_WIDTH + 1, 8) if s >= 1]
                xb = ext_ref[pl.ds(r0 + phase, rows + 24), pl.ds(c0, LANES)]
                for s in shifts:
                    a = s - phase
                    acc = acc + xb[a:a + rows, :] * wdw_ref[pl.ds(s - 1, 1), pl.ds(c0, LANES)]
            acc_ref[pl.ds(r0, rows), pl.ds(c0, LANES)] = acc
        return carry

    lax.fori_loop(0, D_MODEL // LANES, cblock, 0)

    cv = acc_ref[...]
    xc = cv - jnp.mean(cv, axis=-1, keepdims=True)
    var = jnp.mean(xc * xc, axis=-1, keepdims=True)
    ln = xc * lax.rsqrt(var + EPS) * gln_ref[...] + bln_ref[...]
    act = ln * jax.nn.sigmoid(ln)
    y_ref[...] = jnp.dot(act.astype(BF16), wout_ref[...], preferred_element_type=F32).astype(y_ref.dtype)


def _conv_mixer(proj, seq, wdw, bdw, gln, bln, wout):
    t = proj.shape[0]
    tm = _tile(seq, 256)
    hb = tm // HALO
    nhalo = t // HALO
    main = lambda col: pl.BlockSpec((tm, D_MODEL), lambda i: (i, col))
    left = lambda col: pl.BlockSpec((HALO, D_MODEL), lambda i: (jnp.maximum(i * hb - 1, 0), col))
    right = lambda col: pl.BlockSpec((HALO, D_MODEL), lambda i: (jnp.minimum((i + 1) * hb, nhalo - 1), col))
    cv, cg = COL_CVAL // D_MODEL, COL_CGATE // D_MODEL
    return pl.pallas_call(
        functools.partial(_conv_kernel, tiles_per_seq=seq // tm, tm=tm),
        grid=(t // tm,),
        in_specs=[
            main(cv), left(cv), right(cv), main(cg), left(cg), right(cg),
            _const_spec((CONV_WIDTH + 1, D_MODEL)),
            _const_spec((1, D_MODEL)),
            _const_spec((1, D_MODEL)),
            _const_spec((1, D_MODEL)),
            _const_spec((D_MODEL, D_MODEL)),
        ],
        out_specs=pl.BlockSpec((tm, D_MODEL), lambda i: (i, 0)),
        out_shape=jax.ShapeDtypeStruct((t, D_MODEL), BF16),
        scratch_shapes=[pltpu.VMEM((tm + 2 * HALO, D_MODEL), F32), pltpu.VMEM((tm, D_MODEL), F32)],
        compiler_params=_params(("parallel",), 40),
        name="conv_mixer",
    )(proj, proj, proj, proj, proj, proj, wdw, bdw, gln, bln, wout)


def _gla_kernel(q_ref, k_ref, v_ref, z_ref, wa_ref, ba_ref, tri_ref, *rest, reverse, chunk, nchunks):
    if reverse:
        prev_ref, o_ref, s_ref = rest
    else:
        o_ref, s_ref = rest

    @pl.when(pl.program_id(1) == 0)
    def _():
        s_ref[...] = jnp.zeros_like(s_ref)

    half = chunk // 2
    row = lax.broadcasted_iota(jnp.int32, (chunk, chunk), 0)
    col = lax.broadcasted_iota(jnp.int32, (chunk, chunk), 1)
    visible = (col >= row) if reverse else (col <= row)
    edge = 0 if reverse else chunk - 1
    mid = half if reverse else half - 1

    def step(ci, carry):
        cidx = (nchunks - 1 - ci) if reverse else ci
        r0 = pl.multiple_of(cidx * chunk, chunk)
        zc = z_ref[pl.ds(r0, chunk), :]
        for h in range(GLA_HEADS):
            kd = slice(h * GLA_DK, (h + 1) * GLA_DK)
            vd = slice(h * GLA_DV, (h + 1) * GLA_DV)
            pre = jnp.dot(zc, wa_ref[:, kd], preferred_element_type=F32) + ba_ref[:, kd]
            la = (jnp.minimum(pre, 0.0) - jnp.log1p(jnp.exp(-jnp.abs(pre)))) * (1.0 / GATE_TAU)
            hi = la.astype(BF16)
            lo = (la - hi.astype(F32)).astype(BF16)
            cum = jnp.dot(tri_ref[...], jnp.concatenate([hi, lo], axis=0), preferred_element_type=F32)
            tot = cum[edge:edge + 1, :]
            ref = cum[mid:mid + 1, :]
            qf = q_ref[pl.ds(r0, chunk), kd].astype(F32) * (GLA_DK ** -0.5)
            kf = k_ref[pl.ds(r0, chunk), kd].astype(F32)
            q_in = qf * jnp.exp(cum)
            q_t = (q_in * jnp.exp(-ref)).astype(BF16)
            k_t = kf * jnp.exp(ref - cum)
            k_dec = (k_t * jnp.exp(tot - ref)).astype(BF16)
            sc = lax.dot_general(q_t, k_t.astype(BF16), (((1,), (1,)), ((), ())),
                                 preferred_element_type=F32)
            sc = jnp.where(visible, sc, 0.0).astype(BF16)
            vh = v_ref[pl.ds(r0, chunk), vd]
            st = s_ref[h]
            o = jnp.dot(sc, vh, preferred_element_type=F32)
            o = o + lax.dot_general(q_in.astype(BF16), st.astype(BF16), (((1,), (1,)), ((), ())),
                                    preferred_element_type=F32)
            upd = lax.dot_general(vh, k_dec, (((0,), (0,)), ((), ())), preferred_element_type=F32)
            s_ref[h] = st * jnp.exp(tot) + upd
            if reverse:
                o = o + prev_ref[pl.ds(r0, chunk), vd]
            o_ref[pl.ds(r0, chunk), vd] = o
        return carry

    lax.fori_loop(0, nchunks, step, 0)


def _gla_direction(proj, zc, wa, ba, tri, prev, seq, reverse):
    t = proj.shape[0]
    rows = _tile(seq, GLA_ROWS)
    chunk = min(GLA_CHUNK, rows)
    nblk = seq // rows

    def rb(b, c):
        return b * nblk + ((nblk - 1 - c) if reverse else c)

    in_specs = [
        pl.BlockSpec((rows, GLA_KDIM), lambda b, c: (rb(b, c), COL_Q // GLA_KDIM)),
        pl.BlockSpec((rows, GLA_KDIM), lambda b, c: (rb(b, c), COL_K // GLA_KDIM)),
        pl.BlockSpec((rows, D_MODEL), lambda b, c: (rb(b, c), COL_V // D_MODEL)),
        pl.BlockSpec((rows, LANES), lambda b, c: (rb(b, c), 0)),
        _const_spec((LANES, GLA_KDIM)),
        _const_spec((1, GLA_KDIM)),
        _const_spec((chunk, 2 * chunk)),
    ]
    args = [proj, proj, proj, zc, wa, ba, tri]
    if reverse:
        in_specs.append(pl.BlockSpec((rows, D_MODEL), lambda b, c: (rb(b, c), 0)))
        args.append(prev)
    return pl.pallas_call(
        functools.partial(_gla_kernel, reverse=reverse, chunk=chunk, nchunks=rows // chunk),
        grid=(t // seq, nblk),
        in_specs=in_specs,
        out_specs=pl.BlockSpec((rows, D_MODEL), lambda b, c: (rb(b, c), 0)),
        out_shape=jax.ShapeDtypeStruct((t, D_MODEL), F32),
        scratch_shapes=[pltpu.VMEM((GLA_HEADS, GLA_DV, GLA_DK), F32)],
        compiler_params=_params(("parallel", "arbitrary"), 40),
        name="gla_bwd" if reverse else "gla_fwd",
    )(*args)


def _tri(chunk, reverse):
    r = jnp.arange(chunk)[:, None]
    c = jnp.arange(chunk)[None, :]
    m = ((c >= r) if reverse else (c <= r)).astype(BF16)
    return jnp.concatenate([m, m], axis=1)


def _merge_kernel(o_ref, r_ref, sc_ref, sg_ref, yc_ref, x_ref, gg_ref, wg_ref, wo_ref, out_ref, on_ref):
    for h in range(GLA_HEADS):
        vd = slice(h * GLA_DV, (h + 1) * GLA_DV)
        r = r_ref[:, vd].astype(F32)
        on = _rms(o_ref[:, vd], gg_ref[:, vd]) * (r * jax.nn.sigmoid(r))
        on_ref[:, vd] = on.astype(BF16)
    y_gla = jnp.dot(on_ref[...], wg_ref[...], preferred_element_type=F32)
    mixed = (jax.nn.sigmoid(sc_ref[...].astype(F32)) * yc_ref[...].astype(F32)
             + jax.nn.sigmoid(sg_ref[...].astype(F32)) * y_gla)
    out_ref[...] = x_ref[...] + jnp.dot(mixed.astype(BF16), wo_ref[...], preferred_element_type=F32)


def _merge(o, proj, y_conv, x, g_gla, w_gla_out, w_out):
    t = x.shape[0]
    tm = _tile(t, 256)
    blk = lambda col: pl.BlockSpec((tm, D_MODEL), lambda i: (i, col))
    return pl.pallas_call(
        _merge_kernel,
        grid=(t // tm,),
        in_specs=[
            blk(0), blk(COL_R // D_MODEL), blk(COL_SCONV // D_MODEL), blk(COL_SGLA // D_MODEL),
            blk(0), blk(0),
            _const_spec((1, D_MODEL)),
            _const_spec((D_MODEL, D_MODEL)),
            _const_spec((D_MODEL, D_MODEL)),
        ],
        out_specs=blk(0),
        out_shape=jax.ShapeDtypeStruct((t, D_MODEL), F32),
        scratch_shapes=[pltpu.VMEM((tm, D_MODEL), BF16)],
        compiler_params=_params(("parallel",), 48),
        name="merge",
    )(o, proj, proj, proj, y_conv, x, g_gla, w_gla_out, w_out)


def _gelu_tanh(x):
    return 0.5 * x * (1.0 + jnp.tanh(0.7978845608028654 * (x + 0.044715 * (x * x * x))))


def _ffn_kernel(x_ref, xl_ref, xr_ref, g_ref, wg_ref, wv_ref, wdw_ref, bdw_ref, wd_ref, out_ref, h_ref,
                *, tiles_per_seq, tm):
    f = pl.program_id(1)

    @pl.when(f == 0)
    def _():
        t = pl.program_id(0) % tiles_per_seq
        g = g_ref[...]
        x = x_ref[...]
        h_ref[0:HALO, :] = jnp.where(t != 0, _rms(xl_ref[...], g), 0.0).astype(BF16)
        h_ref[HALO:HALO + tm, :] = _rms(x, g).astype(BF16)
        h_ref[HALO + tm:2 * HALO + tm, :] = jnp.where(
            t != tiles_per_seq - 1, _rms(xr_ref[...], g), 0.0).astype(BF16)
        out_ref[...] = x

    gt = jnp.dot(h_ref[...], wg_ref[...], preferred_element_type=F32)
    val = jnp.dot(h_ref[HALO:HALO + tm, :], wv_ref[...], preferred_element_type=F32)
    conv = (gt[HALO - 1:HALO - 1 + tm, :] * wdw_ref[0:1, :]
            + gt[HALO:HALO + tm, :] * wdw_ref[1:2, :]
            + gt[HALO + 1:HALO + 1 + tm, :] * wdw_ref[2:3, :]
            + bdw_ref[...])
    act = (_gelu_tanh(conv) * val).astype(BF16)
    out_ref[...] += jnp.dot(act, wd_ref[...], preferred_element_type=F32)


def _ffn(x, seq, g, w_up, wdw, bdw, w_down):
    t = x.shape[0]
    tm = _tile(seq, 512)
    tf = 512
    nf = FFN_DIM // tf
    hb = tm // HALO
    nhalo = t // HALO
    return pl.pallas_call(
        functools.partial(_ffn_kernel, tiles_per_seq=seq // tm, tm=tm),
        grid=(t // tm, nf),
        in_specs=[
            pl.BlockSpec((tm, D_MODEL), lambda i, f: (i, 0)),
            pl.BlockSpec((HALO, D_MODEL), lambda i, f: (jnp.maximum(i * hb - 1, 0), 0)),
            pl.BlockSpec((HALO, D_MODEL), lambda i, f: (jnp.minimum((i + 1) * hb, nhalo - 1), 0)),
            pl.BlockSpec((1, D_MODEL), lambda i, f: (0, 0)),
            pl.BlockSpec((D_MODEL, tf), lambda i, f: (0, f)),
            pl.BlockSpec((D_MODEL, tf), lambda i, f: (0, f + nf)),
            pl.BlockSpec((8, tf), lambda i, f: (0, f)),
            pl.BlockSpec((1, tf), lambda i, f: (0, f)),
            pl.BlockSpec((tf, D_MODEL), lambda i, f: (f, 0)),
        ],
        out_specs=pl.BlockSpec((tm, D_MODEL), lambda i, f: (i, 0)),
        out_shape=jax.ShapeDtypeStruct((t, D_MODEL), F32),
        scratch_shapes=[pltpu.VMEM((tm + 2 * HALO, D_MODEL), BF16)],
        compiler_params=_params(("parallel", "arbitrary"), 48),
        name="ffn",
    )(x, x, x, g, w_up, w_up, wdw, bdw, w_down)


def _ple_kernel(x_ref, p_ref, g_ref, wg_ref, wp_ref, gf_ref, out_ref, *, final):
    x = x_ref[...]
    gate = jax.nn.sigmoid(jnp.dot(_rms(x, g_ref[...]).astype(BF16), wg_ref[...], preferred_element_type=F32))
    y = x + gate * jnp.dot(p_ref[...].astype(BF16), wp_ref[...], preferred_element_type=F32)
    if final:
        y = _rms(y, gf_ref[...])
    out_ref[...] = y


def _ple(x, p, g, w_gate, w_proj, g_final, final):
    t = x.shape[0]
    tm = _tile(t, 512)
    return pl.pallas_call(
        functools.partial(_ple_kernel, final=final),
        grid=(t // tm,),
        in_specs=[
            pl.BlockSpec((tm, D_MODEL), lambda i: (i, 0)),
            pl.BlockSpec((tm, PLE_DIM), lambda i: (i, 0)),
            _const_spec((1, D_MODEL)),
            _const_spec((D_MODEL, D_MODEL)),
            _const_spec((PLE_DIM, D_MODEL)),
            _const_spec((1, D_MODEL)),
        ],
        out_specs=pl.BlockSpec((tm, D_MODEL), lambda i: (i, 0)),
        out_shape=jax.ShapeDtypeStruct((t, D_MODEL), F32),
        compiler_params=_params(("parallel",), 40),
        name="ple",
    )(x, p, g, w_gate, w_proj, g_final)


def _split_hi_lo(w):
    hi = w.astype(BF16)
    lo = (w - hi.astype(F32)).astype(BF16)
    return hi, lo


def _pack_layer(i, g_mix, w_in, b_in, w_conv_dw, b_conv_dw, g_conv_ln, b_conv_ln, w_conv_out,
                w_alpha_f, b_alpha_f, w_alpha_b, b_alpha_b, g_gla, w_gla_out, w_out,
                g_ffn, w_ffn_up, w_ffn_dw, b_ffn_dw, w_ffn_down, g_ple, w_ple_gate, w_ple_proj):
    row = lambda v: v[i].reshape(1, -1).astype(F32)
    z0 = COL_SCONV
    z1 = z0 + 2 * GATE_RANK
    w, b = w_in[i], b_in[i]
    w_main = jnp.concatenate([w[:, :z0], w[:, z1:]], axis=1).astype(BF16)
    b_main = jnp.concatenate([b[:z0], b[z1:]]).reshape(1, -1)
    wzf, wzb = w[:, z0:z0 + GATE_RANK], w[:, z0 + GATE_RANK:z1]
    pad_w = jnp.zeros((D_MODEL, LANES - 6 * GATE_RANK), F32)
    wz = jnp.concatenate([wzf, wzf, wzf, wzb, wzb, wzb, pad_w], axis=1).astype(BF16)
    bzf, bzb = b[z0:z0 + GATE_RANK], b[z0 + GATE_RANK:z1]
    bz = jnp.concatenate([bzf, bzf, bzf, bzb, bzb, bzb, jnp.zeros((LANES - 6 * GATE_RANK,), F32)]).reshape(1, -1)
    fhi, flo = _split_hi_lo(w_alpha_f[i])
    bhi, blo = _split_hi_lo(w_alpha_b[i])
    zrows = lambda n: jnp.zeros((n, GLA_KDIM), BF16)
    wa_f = jnp.concatenate([fhi, fhi, flo, zrows(LANES - 3 * GATE_RANK)], axis=0)
    wa_b = jnp.concatenate([zrows(3 * GATE_RANK), bhi, bhi, blo, zrows(LANES - 6 * GATE_RANK)], axis=0)
    wdw = jnp.concatenate([w_conv_dw[i], jnp.zeros((1, D_MODEL), F32)], axis=0)
    fdw = jnp.concatenate([w_ffn_dw[i], jnp.zeros((8 - w_ffn_dw.shape[1], FFN_DIM), F32)], axis=0)
    return dict(
        g_mix=row(g_mix), w_main=w_main, b_main=b_main, wz=wz, bz=bz,
        wdw=wdw, bdw=row(b_conv_dw), gln=row(g_conv_ln), bln=row(b_conv_ln),
        w_conv_out=w_conv_out[i].astype(BF16),
        wa_f=wa_f, ba_f=row(b_alpha_f), wa_b=wa_b, ba_b=row(b_alpha_b),
        g_gla=row(g_gla), w_gla_out=w_gla_out[i].astype(BF16), w_out=w_out[i].astype(BF16),
        g_ffn=row(g_ffn), w_ffn_up=w_ffn_up[i].astype(BF16), fdw=fdw, fdb=row(b_ffn_dw),
        w_ffn_down=w_ffn_down[i].astype(BF16),
        g_ple=row(g_ple), w_ple_gate=w_ple_gate[i].astype(BF16), w_ple_proj=w_ple_proj[i].astype(BF16),
    )


def _trunk(x, p, layers, g_final):
    bsz, seq, _ = x.shape
    t = bsz * seq
    x = x.reshape(t, D_MODEL)
    chunk = min(GLA_CHUNK, _tile(seq, GLA_ROWS))
    tri_f, tri_b = _tri(chunk, False), _tri(chunk, True)
    gf = g_final.reshape(1, -1)
    for li, w in enumerate(layers):
        proj, zc = _inproj(x, w["g_mix"], w["w_main"], w["b_main"], w["wz"], w["bz"])
        y_conv = _conv_mixer(proj, seq, w["wdw"], w["bdw"], w["gln"], w["bln"], w["w_conv_out"])
        o = _gla_direction(proj, zc, w["wa_f"], w["ba_f"], tri_f, None, seq, False)
        o = _gla_direction(proj, zc, w["wa_b"], w["ba_b"], tri_b, o, seq, True)
        x = _merge(o, proj, y_conv, x, w["g_gla"], w["w_gla_out"], w["w_out"])
        x = _ffn(x, seq, w["g_ffn"], w["w_ffn_up"], w["fdw"], w["fdb"], w["w_ffn_down"])
        x = _ple(x, p[li].reshape(t, PLE_DIM), w["g_ple"], w["w_ple_gate"], w["w_ple_proj"], gf,
                 li == len(layers) - 1)
    return x.reshape(bsz, seq, D_MODEL)


def kernel(x_prompt, x_sample, p_prompt, p_sample, g_mix, w_in, b_in, w_conv_dw, b_conv_dw, g_conv_ln,
           b_conv_ln, w_conv_out, w_alpha_f, b_alpha_f, w_alpha_b, b_alpha_b, g_gla, w_gla_out, w_out,
           g_ffn, w_ffn_up, w_ffn_dw, b_ffn_dw, w_ffn_down, g_ple, w_ple_gate, w_ple_proj, g_final):
    layers = [
        _pack_layer(i, g_mix, w_in, b_in, w_conv_dw, b_conv_dw, g_conv_ln, b_conv_ln, w_conv_out,
                    w_alpha_f, b_alpha_f, w_alpha_b, b_alpha_b, g_gla, w_gla_out, w_out,
                    g_ffn, w_ffn_up, w_ffn_dw, b_ffn_dw, w_ffn_down, g_ple, w_ple_gate, w_ple_proj)
        for i in range(g_mix.shape[0])
    ]
    y_prompt = _trunk(x_prompt, p_prompt, layers, g_final)
    y_sample = _trunk(x_sample, p_sample, layers, g_final)
    return (y_prompt, y_sample)
```

```python
import functools

import jax
import jax.numpy as jnp
from jax import lax
from jax.experimental import pallas as pl
from jax.experimental.pallas import tpu as pltpu

F32 = jnp.float32
BF16 = jnp.bfloat16

D_MODEL = 2048
CONV_WIDTH = 31
GLA_HEADS = 4
GLA_DK = 256
GLA_DV = 512
GLA_KDIM = GLA_HEADS * GLA_DK
GATE_RANK = 16
GATE_TAU = 16.0
FFN_DIM = 5632
PLE_DIM = 256
EPS = 1e-6

LANES = 128
HALO = 16
MIB = 1024 * 1024

N_MAIN = 7 * D_MODEL
COL_CVAL, COL_CGATE, COL_Q, COL_K, COL_V, COL_R, COL_SCONV, COL_SGLA = (
    0, 2048, 4096, 5120, 6144, 8192, 10240, 12288)

GLA_CHUNK = 128
GLA_ROWS = 512
CONV_HALF = D_MODEL // 2


def _params(sem, vmem_mib):
    return pltpu.CompilerParams(dimension_semantics=sem, vmem_limit_bytes=vmem_mib * MIB)


def _tile(total, pref):
    t = min(total, pref)
    while total % t:
        t //= 2
    return t


def _rms(x, g):
    ms = jnp.mean(x * x, axis=-1, keepdims=True)
    return x * lax.rsqrt(ms + EPS) * g


def _sigmoid(x):
    return 0.5 * jnp.tanh(0.5 * x) + 0.5


def _const_spec(shape):
    return pl.BlockSpec(shape, lambda *_: (0,) * len(shape), pipeline_mode=pl.Buffered(1))


def _inproj_kernel(x_ref, g_ref, w_ref, b_ref, wz_ref, bz_ref, o_ref, z_ref, h_ref):
    @pl.when(pl.program_id(1) == 0)
    def _():
        hb = _rms(x_ref[...], g_ref[...]).astype(BF16)
        h_ref[...] = hb
        z = jnp.dot(hb, wz_ref[...], preferred_element_type=F32) + bz_ref[...]
        hi = z.astype(BF16)
        lo = (z - hi.astype(F32)).astype(BF16)
        lane = lax.broadcasted_iota(jnp.int32, z.shape, 1) % (3 * GATE_RANK)
        is_lo = (lane >= GATE_RANK) & (lane < 2 * GATE_RANK)
        z_ref[...] = jnp.where(is_lo, lo, hi)

    acc = jnp.dot(h_ref[...], w_ref[...], preferred_element_type=F32) + b_ref[...]
    o_ref[...] = acc.astype(o_ref.dtype)


def _inproj(x, g, w, b, wz, bz):
    t = x.shape[0]
    tm = _tile(t, 1024)
    tn = 1024
    return pl.pallas_call(
        _inproj_kernel,
        grid=(t // tm, N_MAIN // tn),
        in_specs=[
            pl.BlockSpec((tm, D_MODEL), lambda i, j: (i, 0)),
            pl.BlockSpec((1, D_MODEL), lambda i, j: (0, 0)),
            pl.BlockSpec((D_MODEL, tn), lambda i, j: (0, j)),
            pl.BlockSpec((1, tn), lambda i, j: (0, j)),
            pl.BlockSpec((D_MODEL, LANES), lambda i, j: (0, 0)),
            pl.BlockSpec((1, LANES), lambda i, j: (0, 0)),
        ],
        out_specs=[
            pl.BlockSpec((tm, tn), lambda i, j: (i, j)),
            pl.BlockSpec((tm, LANES), lambda i, j: (i, 0)),
        ],
        out_shape=[
            jax.ShapeDtypeStruct((t, N_MAIN), BF16),
            jax.ShapeDtypeStruct((t, LANES), BF16),
        ],
        scratch_shapes=[pltpu.VMEM((tm, D_MODEL), BF16)],
        compiler_params=_params(("parallel", "arbitrary"), 48),
        name="inproj",
    )(x, g, w, b, wz, bz)


def _conv_taps(ext_ref, wdw_ref, r0, lanes, rows):
    span = rows + 8
    acc = None
    for phase in range(8):
        part = None
        for s in range(phase, CONV_WIDTH + 1, 8):
            if s < 1:
                continue
            start = pl.multiple_of(r0 + (s - phase), 8)
            term = ext_ref[pl.ds(start, span), lanes] * wdw_ref[s - 1:s, lanes]
            part = term if part is None else part + term
        if phase:
            part = pltpu.roll(part, span - phase, axis=0)
        part = part[:rows, :]
        acc = part if acc is None else acc + part
    return acc


def _gla_kernel(q_ref, k_ref, v_ref, z_ref, wa_ref, ba_ref, tri_ref,
                cv_ref, cvl_ref, cvr_ref, cg_ref, cgl_ref, cgr_ref, wdw_ref, bdw_ref,
                *rest, reverse, chunk, nchunks, nblk):
    if reverse:
        prev_ref, *rest = rest
    o_ref, cvo_ref, s_ref, ext_ref, qin_ref, qt_ref, kt_ref, kdec_ref, dec_ref = rest
    rows = chunk * nchunks
    blk = pl.program_id(1)
    pos = (nblk - 1 - blk) if reverse else blk

    @pl.when(blk == 0)
    def _():
        s_ref[...] = jnp.zeros_like(s_ref)

    def glu(val_ref, gate_ref):
        return val_ref[...].astype(F32) * _sigmoid(gate_ref[...].astype(F32))

    ext_ref[0:HALO, :] = jnp.where(pos != 0, glu(cvl_ref, cgl_ref), 0.0)
    ext_ref[HALO:HALO + rows, :] = glu(cv_ref, cg_ref)
    ext_ref[HALO + rows:2 * HALO + rows, :] = jnp.where(pos != nblk - 1, glu(cvr_ref, cgr_ref), 0.0)

    half = chunk // 2
    conv_per_head = CONV_HALF // LANES // GLA_HEADS
    row = lax.broadcasted_iota(jnp.int32, (chunk, chunk), 0)
    col = lax.broadcasted_iota(jnp.int32, (chunk, chunk), 1)
    visible = (col >= row) if reverse else (col <= row)
    edge = 0 if reverse else chunk - 1
    mid = half if reverse else half - 1

    for c in range(nchunks):
        rs = slice(c * chunk, (c + 1) * chunk)
        pre = jnp.dot(z_ref[rs, :], wa_ref[...], preferred_element_type=F32) + ba_ref[...]
        la = (jnp.minimum(pre, 0.0) - jnp.log(1.0 + jnp.exp(-jnp.abs(pre)))) * (1.0 / GATE_TAU)
        hi = la.astype(BF16)
        lo = (la - hi.astype(F32)).astype(BF16)
        cum = jnp.dot(tri_ref[...], jnp.concatenate([hi, lo], axis=0), preferred_element_type=F32)
        tot = cum[edge:edge + 1, :]
        ref = cum[mid:mid + 1, :]
        q_in = q_ref[rs, :].astype(F32) * (GLA_DK ** -0.5) * jnp.exp(cum)
        k_t = k_ref[rs, :].astype(F32) * jnp.exp(ref - cum)
        qin_ref[rs, :] = q_in.astype(BF16)
        qt_ref[rs, :] = (q_in * jnp.exp(-ref)).astype(BF16)
        kt_ref[rs, :] = k_t.astype(BF16)
        kdec_ref[rs, :] = (k_t * jnp.exp(tot - ref)).astype(BF16)
        dec_ref[c * 8:(c + 1) * 8, :] = jnp.broadcast_to(jnp.exp(tot), (8, GLA_KDIM))

    def step(ci, carry):
        cidx = (nchunks - 1 - ci) if reverse else ci
        r0 = pl.multiple_of(cidx * chunk, chunk)
        d0 = pl.multiple_of(cidx * 8, 8)
        for h in range(GLA_HEADS):
            kd = slice(h * GLA_DK, (h + 1) * GLA_DK)
            vd = slice(h * GLA_DV, (h + 1) * GLA_DV)
            sc = lax.dot_general(qt_ref[pl.ds(r0, chunk), kd], kt_ref[pl.ds(r0, chunk), kd],
                                 (((1,), (1,)), ((), ())), preferred_element_type=F32)
            sc = jnp.where(visible, sc, 0.0).astype(BF16)
            vh = v_ref[pl.ds(r0, chunk), vd]
            st = s_ref[h]
            o = jnp.dot(sc, vh, preferred_element_type=F32)
            o = o + lax.dot_general(qin_ref[pl.ds(r0, chunk), kd], st.astype(BF16), (((1,), (1,)), ((), ())),
                                    preferred_element_type=F32)
            upd = lax.dot_general(vh, kdec_ref[pl.ds(r0, chunk), kd], (((0,), (0,)), ((), ())),
                                  preferred_element_type=F32)
            s_ref[h] = st * dec_ref[pl.ds(d0, 8), kd][0:1, :] + upd
            if reverse:
                o = o + prev_ref[pl.ds(r0, chunk), vd]
            o_ref[pl.ds(r0, chunk), vd] = o
            for cb in range(h * conv_per_head, (h + 1) * conv_per_head):
                lanes = slice(cb * LANES, (cb + 1) * LANES)
                cvo_ref[pl.ds(r0, chunk), lanes] = (
                    _conv_taps(ext_ref, wdw_ref, r0, lanes, chunk) + bdw_ref[:, lanes])
        return carry

    lax.fori_loop(0, nchunks, step, 0)


def _gla_direction(proj, zc, wa, ba, tri, wdw, bdw, prev, seq, reverse):
    t = proj.shape[0]
    rows = _tile(seq, GLA_ROWS)
    chunk = min(GLA_CHUNK, rows)
    nblk = seq // rows
    hb = rows // HALO
    nhalo = t // HALO
    part = 1 if reverse else 0

    def rb(b, c):
        return b * nblk + ((nblk - 1 - c) if reverse else c)

    def conv_specs(col0):
        col = col0 // CONV_HALF + part
        return [
            pl.BlockSpec((rows, CONV_HALF), lambda b, c: (rb(b, c), col)),
            pl.BlockSpec((HALO, CONV_HALF), lambda b, c: (jnp.maximum(rb(b, c) * hb - 1, 0), col)),
            pl.BlockSpec((HALO, CONV_HALF), lambda b, c: (jnp.minimum((rb(b, c) + 1) * hb, nhalo - 1), col)),
        ]

    in_specs = [
        pl.BlockSpec((rows, GLA_KDIM), lambda b, c: (rb(b, c), COL_Q // GLA_KDIM)),
        pl.BlockSpec((rows, GLA_KDIM), lambda b, c: (rb(b, c), COL_K // GLA_KDIM)),
        pl.BlockSpec((rows, D_MODEL), lambda b, c: (rb(b, c), COL_V // D_MODEL)),
        pl.BlockSpec((rows, LANES), lambda b, c: (rb(b, c), 0)),
        _const_spec((LANES, GLA_KDIM)),
        _const_spec((1, GLA_KDIM)),
        _const_spec((chunk, 2 * chunk)),
        *conv_specs(COL_CVAL), *conv_specs(COL_CGATE),
        pl.BlockSpec((CONV_WIDTH + 1, CONV_HALF), lambda b, c: (0, part), pipeline_mode=pl.Buffered(1)),
        pl.BlockSpec((1, CONV_HALF), lambda b, c: (0, part), pipeline_mode=pl.Buffered(1)),
    ]
    args = [proj, proj, proj, zc, wa, ba, tri, proj, proj, proj, proj, proj, proj, wdw, bdw]
    if reverse:
        in_specs.append(pl.BlockSpec((rows, D_MODEL), lambda b, c: (rb(b, c), 0)))
        args.append(prev)
    return pl.pallas_call(
        functools.partial(_gla_kernel, reverse=reverse, chunk=chunk, nchunks=rows // chunk, nblk=nblk),
        grid=(t // seq, nblk),
        in_specs=in_specs,
        out_specs=[
            pl.BlockSpec((rows, D_MODEL), lambda b, c: (rb(b, c), 0)),
            pl.BlockSpec((rows, CONV_HALF), lambda b, c: (rb(b, c), 0)),
        ],
        out_shape=[
            jax.ShapeDtypeStruct((t, D_MODEL), F32),
            jax.ShapeDtypeStruct((t, CONV_HALF), F32),
        ],
        scratch_shapes=[
            pltpu.VMEM((GLA_HEADS, GLA_DV, GLA_DK), F32),
            pltpu.VMEM((rows + 2 * HALO, CONV_HALF), F32),
            pltpu.VMEM((rows, GLA_KDIM), BF16),
            pltpu.VMEM((rows, GLA_KDIM), BF16),
            pltpu.VMEM((rows, GLA_KDIM), BF16),
            pltpu.VMEM((rows, GLA_KDIM), BF16),
            pltpu.VMEM((8 * (rows // chunk), GLA_KDIM), F32),
        ],
        compiler_params=_params(("parallel", "arbitrary"), 56),
        name="gla_bwd" if reverse else "gla_fwd",
    )(*args)


def _tri(chunk, reverse):
    r = jnp.arange(chunk)[:, None]
    c = jnp.arange(chunk)[None, :]
    m = ((c >= r) if reverse else (c <= r)).astype(BF16)
    return jnp.concatenate([m, m], axis=1)


def _merge_kernel(o_ref, r_ref, sc_ref, sg_ref, cva_ref, cvb_ref, x_ref, gg_ref, gln_ref, bln_ref,
                  wc_ref, wg_ref, wo_ref, out_ref, on_ref, act_ref):
    for h in range(GLA_HEADS):
        vd = slice(h * GLA_DV, (h + 1) * GLA_DV)
        r = r_ref[:, vd].astype(F32)
        on = _rms(o_ref[:, vd], gg_ref[:, vd]) * (r * _sigmoid(r))
        on_ref[:, vd] = on.astype(BF16)
    y_gla = jnp.dot(on_ref[...], wg_ref[...], preferred_element_type=F32)

    ca, cb = cva_ref[...], cvb_ref[...]
    lo, hi = slice(0, CONV_HALF), slice(CONV_HALF, D_MODEL)
    mean = (jnp.sum(ca, axis=-1, keepdims=True) + jnp.sum(cb, axis=-1, keepdims=True)) * (1.0 / D_MODEL)
    ca, cb = ca - mean, cb - mean
    var = (jnp.sum(ca * ca, axis=-1, keepdims=True) + jnp.sum(cb * cb, axis=-1, keepdims=True)) * (1.0 / D_MODEL)
    inv = lax.rsqrt(var + EPS)
    for part, sl in ((ca, lo), (cb, hi)):
        ln = part * inv * gln_ref[:, sl] + bln_ref[:, sl]
        act_ref[:, sl] = (ln * _sigmoid(ln)).astype(BF16)
    y_conv = jnp.dot(act_ref[...], wc_ref[...], preferred_element_type=F32)

    mixed = (_sigmoid(sc_ref[...].astype(F32)) * y_conv
             + _sigmoid(sg_ref[...].astype(F32)) * y_gla)
    out_ref[...] = x_ref[...] + jnp.dot(mixed.astype(BF16), wo_ref[...], preferred_element_type=F32)


def _merge(o, proj, cv_a, cv_b, x, g_gla, gln, bln, w_conv_out, w_gla_out, w_out):
    t = x.shape[0]
    tm = _tile(t, 256)
    blk = lambda col: pl.BlockSpec((tm, D_MODEL), lambda i: (i, col))
    half = pl.BlockSpec((tm, CONV_HALF), lambda i: (i, 0))
    return pl.pallas_call(
        _merge_kernel,
        grid=(t // tm,),
        in_specs=[
            blk(0), blk(COL_R // D_MODEL), blk(COL_SCONV // D_MODEL), blk(COL_SGLA // D_MODEL),
            half, half, blk(0),
            _const_spec((1, D_MODEL)), _const_spec((1, D_MODEL)), _const_spec((1, D_MODEL)),
            _const_spec((D_MODEL, D_MODEL)),
            _const_spec((D_MODEL, D_MODEL)),
            _const_spec((D_MODEL, D_MODEL)),
        ],
        out_specs=blk(0),
        out_shape=jax.ShapeDtypeStruct((t, D_MODEL), F32),
        scratch_shapes=[pltpu.VMEM((tm, D_MODEL), BF16), pltpu.VMEM((tm, D_MODEL), BF16)],
        compiler_params=_params(("parallel",), 56),
        name="merge",
    )(o, proj, proj, proj, cv_a, cv_b, x, g_gla, gln, bln, w_conv_out, w_gla_out, w_out)


def _gelu_tanh(x):
    return 0.5 * x * (1.0 + jnp.tanh(0.7978845608028654 * (x + 0.044715 * (x * x * x))))


def _ffn_kernel(x_ref, xl_ref, xr_ref, g_ref, wg_ref, wv_ref, wdw_ref, bdw_ref, wd_ref, out_ref, h_ref,
                *, tiles_per_seq, tm):
    f = pl.program_id(1)

    @pl.when(f == 0)
    def _():
        t = pl.program_id(0) % tiles_per_seq
        g = g_ref[...]
        x = x_ref[...]
        h_ref[0:HALO, :] = jnp.where(t != 0, _rms(xl_ref[...], g), 0.0).astype(BF16)
        h_ref[HALO:HALO + tm, :] = _rms(x, g).astype(BF16)
        h_ref[HALO + tm:2 * HALO + tm, :] = jnp.where(
            t != tiles_per_seq - 1, _rms(xr_ref[...], g), 0.0).astype(BF16)
        out_ref[...] = x

    gt = jnp.dot(h_ref[...], wg_ref[...], preferred_element_type=F32)
    val = jnp.dot(h_ref[HALO:HALO + tm, :], wv_ref[...], preferred_element_type=F32)
    conv = (gt[HALO - 1:HALO - 1 + tm, :] * wdw_ref[0:1, :]
            + gt[HALO:HALO + tm, :] * wdw_ref[1:2, :]
            + gt[HALO + 1:HALO + 1 + tm, :] * wdw_ref[2:3, :]
            + bdw_ref[...])
    act = (_gelu_tanh(conv) * val).astype(BF16)
    out_ref[...] += jnp.dot(act, wd_ref[...], preferred_element_type=F32)


def _ffn(x, seq, g, w_up, wdw, bdw, w_down):
    t = x.shape[0]
    tm = _tile(seq, 512)
    tf = 512
    nf = FFN_DIM // tf
    hb = tm // HALO
    nhalo = t // HALO
    return pl.pallas_call(
        functools.partial(_ffn_kernel, tiles_per_seq=seq // tm, tm=tm),
        grid=(t // tm, nf),
        in_specs=[
            pl.BlockSpec((tm, D_MODEL), lambda i, f: (i, 0)),
            pl.BlockSpec((HALO, D_MODEL), lambda i, f: (jnp.maximum(i * hb - 1, 0), 0)),
            pl.BlockSpec((HALO, D_MODEL), lambda i, f: (jnp.minimum((i + 1) * hb, nhalo - 1), 0)),
            pl.BlockSpec((1, D_MODEL), lambda i, f: (0, 0)),
            pl.BlockSpec((D_MODEL, tf), lambda i, f: (0, f)),
            pl.BlockSpec((D_MODEL, tf), lambda i, f: (0, f + nf)),
            pl.BlockSpec((8, tf), lambda i, f: (0, f)),
            pl.BlockSpec((1, tf), lambda i, f: (0, f)),
            pl.BlockSpec((tf, D_MODEL), lambda i, f: (f, 0)),
        ],
        out_specs=pl.BlockSpec((tm, D_MODEL), lambda i, f: (i, 0)),
        out_shape=jax.ShapeDtypeStruct((t, D_MODEL), F32),
        scratch_shapes=[pltpu.VMEM((tm + 2 * HALO, D_MODEL), BF16)],
        compiler_params=_params(("parallel", "arbitrary"), 48),
        name="ffn",
    )(x, x, x, g, w_up, w_up, wdw, bdw, w_down)


def _ple_kernel(x_ref, p_ref, g_ref, wg_ref, wp_ref, gf_ref, out_ref, *, final):
    x = x_ref[...]
    gate = _sigmoid(jnp.dot(_rms(x, g_ref[...]).astype(BF16), wg_ref[...], preferred_element_type=F32))
    y = x + gate * jnp.dot(p_ref[...].astype(BF16), wp_ref[...], preferred_element_type=F32)
    if final:
        y = _rms(y, gf_ref[...])
    out_ref[...] = y


def _ple(x, p, g, w_gate, w_proj, g_final, final):
    t = x.shape[0]
    tm = _tile(t, 512)
    return pl.pallas_call(
        functools.partial(_ple_kernel, final=final),
        grid=(t // tm,),
        in_specs=[
            pl.BlockSpec((tm, D_MODEL), lambda i: (i, 0)),
            pl.BlockSpec((tm, PLE_DIM), lambda i: (i, 0)),
            _const_spec((1, D_MODEL)),
            _const_spec((D_MODEL, D_MODEL)),
            _const_spec((PLE_DIM, D_MODEL)),
            _const_spec((1, D_MODEL)),
        ],
        out_specs=pl.BlockSpec((tm, D_MODEL), lambda i: (i, 0)),
        out_shape=jax.ShapeDtypeStruct((t, D_MODEL), F32),
        compiler_params=_params(("parallel",), 40),
        name="ple",
    )(x, p, g, w_gate, w_proj, g_final)


def _split_hi_lo(w):
    hi = w.astype(BF16)
    lo = (w - hi.astype(F32)).astype(BF16)
    return hi, lo


def _pack_layer(i, g_mix, w_in, b_in, w_conv_dw, b_conv_dw, g_conv_ln, b_conv_ln, w_conv_out,
                w_alpha_f, b_alpha_f, w_alpha_b, b_alpha_b, g_gla, w_gla_out, w_out,
                g_ffn, w_ffn_up, w_ffn_dw, b_ffn_dw, w_ffn_down, g_ple, w_ple_gate, w_ple_proj):
    row = lambda v: v[i].reshape(1, -1).astype(F32)
    z0 = COL_SCONV
    z1 = z0 + 2 * GATE_RANK
    w, b = w_in[i], b_in[i]
    w_main = jnp.concatenate([w[:, :z0], w[:, z1:]], axis=1).astype(BF16)
    b_main = jnp.concatenate([b[:z0], b[z1:]]).reshape(1, -1)
    wzf, wzb = w[:, z0:z0 + GATE_RANK], w[:, z0 + GATE_RANK:z1]
    pad_w = jnp.zeros((D_MODEL, LANES - 6 * GATE_RANK), F32)
    wz = jnp.concatenate([wzf, wzf, wzf, wzb, wzb, wzb, pad_w], axis=1).astype(BF16)
    bzf, bzb = b[z0:z0 + GATE_RANK], b[z0 + GATE_RANK:z1]
    bz = jnp.concatenate([bzf, bzf, bzf, bzb, bzb, bzb, jnp.zeros((LANES - 6 * GATE_RANK,), F32)]).reshape(1, -1)
    fhi, flo = _split_hi_lo(w_alpha_f[i])
    bhi, blo = _split_hi_lo(w_alpha_b[i])
    zrows = lambda n: jnp.zeros((n, GLA_KDIM), BF16)
    wa_f = jnp.concatenate([fhi, fhi, flo, zrows(LANES - 3 * GATE_RANK)], axis=0)
    wa_b = jnp.concatenate([zrows(3 * GATE_RANK), bhi, bhi, blo, zrows(LANES - 6 * GATE_RANK)], axis=0)
    wdw = jnp.concatenate([w_conv_dw[i], jnp.zeros((1, D_MODEL), F32)], axis=0)
    fdw = jnp.concatenate([w_ffn_dw[i], jnp.zeros((8 - w_ffn_dw.shape[1], FFN_DIM), F32)], axis=0)
    return dict(
        g_mix=row(g_mix), w_main=w_main, b_main=b_main, wz=wz, bz=bz,
        wdw=wdw, bdw=row(b_conv_dw), gln=row(g_conv_ln), bln=row(b_conv_ln),
        w_conv_out=w_conv_out[i].astype(BF16),
        wa_f=wa_f, ba_f=row(b_alpha_f), wa_b=wa_b, ba_b=row(b_alpha_b),
        g_gla=row(g_gla), w_gla_out=w_gla_out[i].astype(BF16), w_out=w_out[i].astype(BF16),
        g_ffn=row(g_ffn), w_ffn_up=w_ffn_up[i].astype(BF16), fdw=fdw, fdb=row(b_ffn_dw),
        w_ffn_down=w_ffn_down[i].astype(BF16),
        g_ple=row(g_ple), w_ple_gate=w_ple_gate[i].astype(BF16), w_ple_proj=w_ple_proj[i].astype(BF16),
    )


def _trunk(x, p, layers, g_final):
    bsz, seq, _ = x.shape
    t = bsz * seq
    x = x.reshape(t, D_MODEL)
    chunk = min(GLA_CHUNK, _tile(seq, GLA_ROWS))
    tri_f, tri_b = _tri(chunk, False), _tri(chunk, True)
    gf = g_final.reshape(1, -1)
    for li, w in enumerate(layers):
        proj, zc = _inproj(x, w["g_mix"], w["w_main"], w["b_main"], w["wz"], w["bz"])
        o, cv_a = _gla_direction(proj, zc, w["wa_f"], w["ba_f"], tri_f, w["wdw"], w["bdw"], None, seq, False)
        o, cv_b = _gla_direction(proj, zc, w["wa_b"], w["ba_b"], tri_b, w["wdw"], w["bdw"], o, seq, True)
        x = _merge(o, proj, cv_a, cv_b, x, w["g_gla"], w["gln"], w["bln"],
                   w["w_conv_out"], w["w_gla_out"], w["w_out"])
        x = _ffn(x, seq, w["g_ffn"], w["w_ffn_up"], w["fdw"], w["fdb"], w["w_ffn_down"])
        x = _ple(x, p[li].reshape(t, PLE_DIM), w["g_ple"], w["w_ple_gate"], w["w_ple_proj"], gf,
                 li == len(layers) - 1)
    return x.reshape(bsz, seq, D_MODEL)


def kernel(x_prompt, x_sample, p_prompt, p_sample, g_mix, w_in, b_in, w_conv_dw, b_conv_dw, g_conv_ln,
           b_conv_ln, w_conv_out, w_alpha_f, b_alpha_f, w_alpha_b, b_alpha_b, g_gla, w_gla_out, w_out,
           g_ffn, w_ffn_up, w_ffn_dw, b_ffn_dw, w_ffn_down, g_ple, w_ple_gate, w_ple_proj, g_final):
    layers = [
        _pack_layer(i, g_mix, w_in, b_in, w_conv_dw, b_conv_dw, g_conv_ln, b_conv_ln, w_conv_out,
                    w_alpha_f, b_alpha_f, w_alpha_b, b_alpha_b, g_gla, w_gla_out, w_out,
                    g_ffn, w_ffn_up, w_ffn_dw, b_ffn_dw, w_ffn_down, g_ple, w_ple_gate, w_ple_proj)
        for i in range(g_mix.shape[0])
    ]
    y_prompt = _trunk(x_prompt, p_prompt, layers, g_final)
    y_sample = _trunk(x_sample, p_sample, layers, g_final)
    return (y_prompt, y_sample)
```

```python
import functools

import jax
import jax.numpy as jnp
from jax import lax
from jax.experimental import pallas as pl
from jax.experimental.pallas import tpu as pltpu

F32 = jnp.float32
BF16 = jnp.bfloat16

D_MODEL = 2048
CONV_WIDTH = 31
GLA_HEADS = 4
GLA_DK = 256
GLA_DV = 512
GLA_KDIM = GLA_HEADS * GLA_DK
GATE_RANK = 16
GATE_TAU = 16.0
FFN_DIM = 5632
PLE_DIM = 256
EPS = 1e-6

LANES = 128
HALO = 16
MIB = 1024 * 1024

N_MAIN = 7 * D_MODEL
COL_CVAL, COL_CGATE, COL_Q, COL_K, COL_V, COL_R, COL_SCONV, COL_SGLA = (
    0, 2048, 4096, 5120, 6144, 8192, 10240, 12288)

GLA_CHUNK = 128
GLA_ROWS = 512
CONV_HALF = D_MODEL // 2


def _params(sem, vmem_mib):
    return pltpu.CompilerParams(dimension_semantics=sem, vmem_limit_bytes=vmem_mib * MIB)


def _tile(total, pref):
    t = min(total, pref)
    while total % t:
        t //= 2
    return t


def _rms(x, g):
    ms = jnp.mean(x * x, axis=-1, keepdims=True)
    return x * lax.rsqrt(ms + EPS) * g


def _sigmoid(x):
    return 0.5 * jnp.tanh(0.5 * x) + 0.5


def _const_spec(shape):
    return pl.BlockSpec(shape, lambda *_: (0,) * len(shape), pipeline_mode=pl.Buffered(1))


def _inproj_kernel(x_ref, g_ref, w_ref, b_ref, wz_ref, bz_ref, o_ref, z_ref, h_ref):
    @pl.when(pl.program_id(1) == 0)
    def _():
        hb = _rms(x_ref[...], g_ref[...]).astype(BF16)
        h_ref[...] = hb
        z = jnp.dot(hb, wz_ref[...], preferred_element_type=F32) + bz_ref[...]
        hi = z.astype(BF16)
        lo = (z - hi.astype(F32)).astype(BF16)
        lane = lax.broadcasted_iota(jnp.int32, z.shape, 1) % (3 * GATE_RANK)
        is_lo = (lane >= GATE_RANK) & (lane < 2 * GATE_RANK)
        z_ref[...] = jnp.where(is_lo, lo, hi)

    acc = jnp.dot(h_ref[...], w_ref[...], preferred_element_type=F32) + b_ref[...]
    o_ref[...] = acc.astype(o_ref.dtype)


def _inproj(x, g, w, b, wz, bz):
    t = x.shape[0]
    tm = _tile(t, 1024)
    tn = 2048
    return pl.pallas_call(
        _inproj_kernel,
        grid=(t // tm, N_MAIN // tn),
        in_specs=[
            pl.BlockSpec((tm, D_MODEL), lambda i, j: (i, 0)),
            pl.BlockSpec((1, D_MODEL), lambda i, j: (0, 0)),
            pl.BlockSpec((D_MODEL, tn), lambda i, j: (0, j)),
            pl.BlockSpec((1, tn), lambda i, j: (0, j)),
            pl.BlockSpec((D_MODEL, LANES), lambda i, j: (0, 0)),
            pl.BlockSpec((1, LANES), lambda i, j: (0, 0)),
        ],
        out_specs=[
            pl.BlockSpec((tm, tn), lambda i, j: (i, j)),
            pl.BlockSpec((tm, LANES), lambda i, j: (i, 0)),
        ],
        out_shape=[
            jax.ShapeDtypeStruct((t, N_MAIN), BF16),
            jax.ShapeDtypeStruct((t, LANES), BF16),
        ],
        scratch_shapes=[pltpu.VMEM((tm, D_MODEL), BF16)],
        compiler_params=_params(("parallel", "arbitrary"), 56),
        name="inproj",
    )(x, g, w, b, wz, bz)


def _conv_taps(ext_ref, wdw_ref, r0, lanes, rows):
    span = rows + 8
    acc = None
    for phase in range(8):
        part = None
        for s in range(phase, CONV_WIDTH + 1, 8):
            if s < 1:
                continue
            start = pl.multiple_of(r0 + (s - phase), 8)
            term = ext_ref[pl.ds(start, span), lanes] * wdw_ref[s - 1:s, lanes]
            part = term if part is None else part + term
        if phase:
            part = pltpu.roll(part, span - phase, axis=0)
        part = part[:rows, :]
        acc = part if acc is None else acc + part
    return acc


def _gla_kernel(q_ref, k_ref, v_ref, z_ref, wa_ref, tri_ref,
                cv_ref, cvl_ref, cvr_ref, cg_ref, cgl_ref, cgr_ref, wdw_ref, bdw_ref,
                *rest, reverse, chunk, nchunks, nblk):
    if reverse:
        prev_ref, *rest = rest
    o_ref, cvo_ref, s_ref, ext_ref, qin_ref, qt_ref, kt_ref, kdec_ref, dec_ref = rest
    rows = chunk * nchunks
    blk = pl.program_id(1)
    pos = (nblk - 1 - blk) if reverse else blk

    @pl.when(blk == 0)
    def _():
        s_ref[...] = jnp.zeros_like(s_ref)

    def glu(val_ref, gate_ref):
        return val_ref[...].astype(F32) * _sigmoid(gate_ref[...].astype(F32))

    ext_ref[0:HALO, :] = jnp.where(pos != 0, glu(cvl_ref, cgl_ref), 0.0)
    ext_ref[HALO:HALO + rows, :] = glu(cv_ref, cg_ref)
    ext_ref[HALO + rows:2 * HALO + rows, :] = jnp.where(pos != nblk - 1, glu(cvr_ref, cgr_ref), 0.0)

    half = chunk // 2
    conv_per_head = CONV_HALF // LANES // GLA_HEADS
    row = lax.broadcasted_iota(jnp.int32, (chunk, chunk), 0)
    col = lax.broadcasted_iota(jnp.int32, (chunk, chunk), 1)
    visible = (col >= row) if reverse else (col <= row)
    edge = 0 if reverse else chunk - 1
    mid = half if reverse else half - 1

    for c in range(nchunks):
        rs = slice(c * chunk, (c + 1) * chunk)
        pre = jnp.dot(z_ref[rs, :], wa_ref[...], preferred_element_type=F32)
        la = (jnp.minimum(pre, 0.0) - jnp.log(1.0 + jnp.exp(-jnp.abs(pre)))) * (1.0 / GATE_TAU)
        hi = la.astype(BF16)
        lo = (la - hi.astype(F32)).astype(BF16)
        cum = jnp.dot(tri_ref[...], jnp.concatenate([hi, lo], axis=0), preferred_element_type=F32)
        tot = cum[edge:edge + 1, :]
        ref = cum[mid:mid + 1, :]
        q_in = q_ref[rs, :].astype(F32) * jnp.exp(cum)
        k_t = k_ref[rs, :].astype(F32) * jnp.exp(ref - cum)
        qin_ref[rs, :] = q_in.astype(BF16)
        qt_ref[rs, :] = (q_in * jnp.exp(-ref)).astype(BF16)
        kt_ref[rs, :] = k_t.astype(BF16)
        kdec_ref[rs, :] = (k_t * jnp.exp(tot - ref)).astype(BF16)
        dec_ref[c * 8:(c + 1) * 8, :] = jnp.broadcast_to(jnp.exp(tot), (8, GLA_KDIM))

    def step(ci, carry):
        cidx = (nchunks - 1 - ci) if reverse else ci
        r0 = pl.multiple_of(cidx * chunk, chunk)
        d0 = pl.multiple_of(cidx * 8, 8)
        for h in range(GLA_HEADS):
            kd = slice(h * GLA_DK, (h + 1) * GLA_DK)
            vd = slice(h * GLA_DV, (h + 1) * GLA_DV)
            sc = lax.dot_general(qt_ref[pl.ds(r0, chunk), kd], kt_ref[pl.ds(r0, chunk), kd],
                                 (((1,), (1,)), ((), ())), preferred_element_type=F32)
            sc = jnp.where(visible, sc, 0.0).astype(BF16)
            vh = v_ref[pl.ds(r0, chunk), vd]
            st = s_ref[h]
            o = jnp.dot(sc, vh, preferred_element_type=F32)
            o = o + lax.dot_general(qin_ref[pl.ds(r0, chunk), kd], st.astype(BF16), (((1,), (1,)), ((), ())),
                                    preferred_element_type=F32)
            upd = lax.dot_general(vh, kdec_ref[pl.ds(r0, chunk), kd], (((0,), (0,)), ((), ())),
                                  preferred_element_type=F32)
            s_ref[h] = st * dec_ref[pl.ds(d0, 8), kd][0:1, :] + upd
            if reverse:
                o = o + prev_ref[pl.ds(r0, chunk), vd]
            o_ref[pl.ds(r0, chunk), vd] = o
            for cb in range(h * conv_per_head, (h + 1) * conv_per_head):
                lanes = slice(cb * LANES, (cb + 1) * LANES)
                cvo_ref[pl.ds(r0, chunk), lanes] = (
                    _conv_taps(ext_ref, wdw_ref, r0, lanes, chunk) + bdw_ref[:, lanes])
        return carry

    lax.fori_loop(0, nchunks, step, 0)


def _gla_direction(proj, zc, wa, tri, wdw, bdw, prev, seq, reverse):
    t = proj.shape[0]
    rows = _tile(seq, GLA_ROWS)
    chunk = min(GLA_CHUNK, rows)
    nblk = seq // rows
    hb = rows // HALO
    nhalo = t // HALO
    part = 1 if reverse else 0

    def rb(b, c):
        return b * nblk + ((nblk - 1 - c) if reverse else c)

    def conv_specs(col0):
        col = col0 // CONV_HALF + part
        return [
            pl.BlockSpec((rows, CONV_HALF), lambda b, c: (rb(b, c), col)),
            pl.BlockSpec((HALO, CONV_HALF), lambda b, c: (jnp.maximum(rb(b, c) * hb - 1, 0), col)),
            pl.BlockSpec((HALO, CONV_HALF), lambda b, c: (jnp.minimum((rb(b, c) + 1) * hb, nhalo - 1), col)),
        ]

    in_specs = [
        pl.BlockSpec((rows, GLA_KDIM), lambda b, c: (rb(b, c), COL_Q // GLA_KDIM)),
        pl.BlockSpec((rows, GLA_KDIM), lambda b, c: (rb(b, c), COL_K // GLA_KDIM)),
        pl.BlockSpec((rows, D_MODEL), lambda b, c: (rb(b, c), COL_V // D_MODEL)),
        pl.BlockSpec((rows, LANES), lambda b, c: (rb(b, c), 0)),
        _const_spec((LANES, GLA_KDIM)),
        _const_spec((chunk, 2 * chunk)),
        *conv_specs(COL_CVAL), *conv_specs(COL_CGATE),
        pl.BlockSpec((CONV_WIDTH + 1, CONV_HALF), lambda b, c: (0, part), pipeline_mode=pl.Buffered(1)),
        pl.BlockSpec((1, CONV_HALF), lambda b, c: (0, part), pipeline_mode=pl.Buffered(1)),
    ]
    args = [proj, proj, proj, zc, wa, tri, proj, proj, proj, proj, proj, proj, wdw, bdw]
    if reverse:
        in_specs.append(pl.BlockSpec((rows, D_MODEL), lambda b, c: (rb(b, c), 0)))
        args.append(prev)
    return pl.pallas_call(
        functools.partial(_gla_kernel, reverse=reverse, chunk=chunk, nchunks=rows // chunk, nblk=nblk),
        grid=(t // seq, nblk),
        in_specs=in_specs,
        out_specs=[
            pl.BlockSpec((rows, D_MODEL), lambda b, c: (rb(b, c), 0)),
            pl.BlockSpec((rows, CONV_HALF), lambda b, c: (rb(b, c), 0)),
        ],
        out_shape=[
            jax.ShapeDtypeStruct((t, D_MODEL), F32),
            jax.ShapeDtypeStruct((t, CONV_HALF), F32),
        ],
        scratch_shapes=[
            pltpu.VMEM((GLA_HEADS, GLA_DV, GLA_DK), F32),
            pltpu.VMEM((rows + 2 * HALO, CONV_HALF), F32),
            pltpu.VMEM((rows, GLA_KDIM), BF16),
            pltpu.VMEM((rows, GLA_KDIM), BF16),
            pltpu.VMEM((rows, GLA_KDIM), BF16),
            pltpu.VMEM((rows, GLA_KDIM), BF16),
            pltpu.VMEM((8 * (rows // chunk), GLA_KDIM), F32),
        ],
        compiler_params=_params(("parallel", "arbitrary"), 56),
        name="gla_bwd" if reverse else "gla_fwd",
    )(*args)


def _tri(chunk, reverse):
    r = jnp.arange(chunk)[:, None]
    c = jnp.arange(chunk)[None, :]
    m = ((c >= r) if reverse else (c <= r)).astype(BF16)
    return jnp.concatenate([m, m], axis=1)


def _merge_kernel(o_ref, r_ref, sc_ref, sg_ref, cva_ref, cvb_ref, x_ref, gg_ref, gln_ref, bln_ref,
                  wc_ref, wg_ref, wo_ref, out_ref, on_ref, act_ref):
    for h in range(GLA_HEADS):
        vd = slice(h * GLA_DV, (h + 1) * GLA_DV)
        r = r_ref[:, vd].astype(F32)
        on = _rms(o_ref[:, vd], gg_ref[:, vd]) * (r * _sigmoid(r))
        on_ref[:, vd] = on.astype(BF16)
    y_gla = jnp.dot(on_ref[...], wg_ref[...], preferred_element_type=F32)

    ca, cb = cva_ref[...], cvb_ref[...]
    lo, hi = slice(0, CONV_HALF), slice(CONV_HALF, D_MODEL)
    mean = (jnp.sum(ca, axis=-1, keepdims=True) + jnp.sum(cb, axis=-1, keepdims=True)) * (1.0 / D_MODEL)
    ca, cb = ca - mean, cb - mean
    var = (jnp.sum(ca * ca, axis=-1, keepdims=True) + jnp.sum(cb * cb, axis=-1, keepdims=True)) * (1.0 / D_MODEL)
    inv = lax.rsqrt(var + EPS)
    for part, sl in ((ca, lo), (cb, hi)):
        ln = part * inv * gln_ref[:, sl] + bln_ref[:, sl]
        act_ref[:, sl] = (ln * _sigmoid(ln)).astype(BF16)
    y_conv = jnp.dot(act_ref[...], wc_ref[...], preferred_element_type=F32)

    mixed = (_sigmoid(sc_ref[...].astype(F32)) * y_conv
             + _sigmoid(sg_ref[...].astype(F32)) * y_gla)
    out_ref[...] = x_ref[...] + jnp.dot(mixed.astype(BF16), wo_ref[...], preferred_element_type=F32)


def _merge(o, proj, cv_a, cv_b, x, g_gla, gln, bln, w_conv_out, w_gla_out, w_out):
    t = x.shape[0]
    tm = _tile(t, 256)
    blk = lambda col: pl.BlockSpec((tm, D_MODEL), lambda i: (i, col))
    half = pl.BlockSpec((tm, CONV_HALF), lambda i: (i, 0))
    return pl.pallas_call(
        _merge_kernel,
        grid=(t // tm,),
        in_specs=[
            blk(0), blk(COL_R // D_MODEL), blk(COL_SCONV // D_MODEL), blk(COL_SGLA // D_MODEL),
            half, half, blk(0),
            _const_spec((1, D_MODEL)), _const_spec((1, D_MODEL)), _const_spec((1, D_MODEL)),
            _const_spec((D_MODEL, D_MODEL)),
            _const_spec((D_MODEL, D_MODEL)),
            _const_spec((D_MODEL, D_MODEL)),
        ],
        out_specs=blk(0),
        out_shape=jax.ShapeDtypeStruct((t, D_MODEL), F32),
        scratch_shapes=[pltpu.VMEM((tm, D_MODEL), BF16), pltpu.VMEM((tm, D_MODEL), BF16)],
        compiler_params=_params(("parallel",), 56),
        name="merge",
    )(o, proj, proj, proj, cv_a, cv_b, x, g_gla, gln, bln, w_conv_out, w_gla_out, w_out)


def _gelu_tanh(x):
    return 0.5 * x * (1.0 + jnp.tanh(0.7978845608028654 * (x + 0.044715 * (x * x * x))))


def _ffn_kernel(x_ref, xl_ref, xr_ref, g_ref, wg_ref, wv_ref, wdw_ref, bdw_ref, wd_ref, out_ref, h_ref,
                *, tiles_per_seq, tm):
    f = pl.program_id(1)

    @pl.when(f == 0)
    def _():
        t = pl.program_id(0) % tiles_per_seq
        g = g_ref[...]
        x = x_ref[...]
        h_ref[0:HALO, :] = jnp.where(t != 0, _rms(xl_ref[...], g), 0.0).astype(BF16)
        h_ref[HALO:HALO + tm, :] = _rms(x, g).astype(BF16)
        h_ref[HALO + tm:2 * HALO + tm, :] = jnp.where(
            t != tiles_per_seq - 1, _rms(xr_ref[...], g), 0.0).astype(BF16)
        out_ref[...] = x

    gt = jnp.dot(h_ref[...], wg_ref[...], preferred_element_type=F32)
    val = jnp.dot(h_ref[HALO:HALO + tm, :], wv_ref[...], preferred_element_type=F32)
    conv = (gt[HALO - 1:HALO - 1 + tm, :] * wdw_ref[0:1, :]
            + gt[HALO:HALO + tm, :] * wdw_ref[1:2, :]
            + gt[HALO + 1:HALO + 1 + tm, :] * wdw_ref[2:3, :]
            + bdw_ref[...])
    act = (_gelu_tanh(conv) * val).astype(BF16)
    out_ref[...] += jnp.dot(act, wd_ref[...], preferred_element_type=F32)


def _ffn(x, seq, g, w_up, wdw, bdw, w_down):
    t = x.shape[0]
    tm = _tile(seq, 1024)
    tf = 512
    nf = FFN_DIM // tf
    hb = tm // HALO
    nhalo = t // HALO
    return pl.pallas_call(
        functools.partial(_ffn_kernel, tiles_per_seq=seq // tm, tm=tm),
        grid=(t // tm, nf),
        in_specs=[
            pl.BlockSpec((tm, D_MODEL), lambda i, f: (i, 0)),
            pl.BlockSpec((HALO, D_MODEL), lambda i, f: (jnp.maximum(i * hb - 1, 0), 0)),
            pl.BlockSpec((HALO, D_MODEL), lambda i, f: (jnp.minimum((i + 1) * hb, nhalo - 1), 0)),
            pl.BlockSpec((1, D_MODEL), lambda i, f: (0, 0)),
            pl.BlockSpec((D_MODEL, tf), lambda i, f: (0, f)),
            pl.BlockSpec((D_MODEL, tf), lambda i, f: (0, f + nf)),
            pl.BlockSpec((8, tf), lambda i, f: (0, f)),
            pl.BlockSpec((1, tf), lambda i, f: (0, f)),
            pl.BlockSpec((tf, D_MODEL), lambda i, f: (f, 0)),
        ],
        out_specs=pl.BlockSpec((tm, D_MODEL), lambda i, f: (i, 0)),
        out_shape=jax.ShapeDtypeStruct((t, D_MODEL), F32),
        scratch_shapes=[pltpu.VMEM((tm + 2 * HALO, D_MODEL), BF16)],
        compiler_params=_params(("parallel", "arbitrary"), 56),
        name="ffn",
    )(x, x, x, g, w_up, w_up, wdw, bdw, w_down)


def _ple_kernel(x_ref, p_ref, g_ref, wg_ref, wp_ref, gf_ref, out_ref, *, final):
    x = x_ref[...]
    gate = _sigmoid(jnp.dot(_rms(x, g_ref[...]).astype(BF16), wg_ref[...], preferred_element_type=F32))
    y = x + gate * jnp.dot(p_ref[...].astype(BF16), wp_ref[...], preferred_element_type=F32)
    if final:
        y = _rms(y, gf_ref[...])
    out_ref[...] = y


def _ple(x, p, g, w_gate, w_proj, g_final, final):
    t = x.shape[0]
    tm = _tile(t, 512)
    return pl.pallas_call(
        functools.partial(_ple_kernel, final=final),
        grid=(t // tm,),
        in_specs=[
            pl.BlockSpec((tm, D_MODEL), lambda i: (i, 0)),
            pl.BlockSpec((tm, PLE_DIM), lambda i: (i, 0)),
            _const_spec((1, D_MODEL)),
            _const_spec((D_MODEL, D_MODEL)),
            _const_spec((PLE_DIM, D_MODEL)),
            _const_spec((1, D_MODEL)),
        ],
        out_specs=pl.BlockSpec((tm, D_MODEL), lambda i: (i, 0)),
        out_shape=jax.ShapeDtypeStruct((t, D_MODEL), F32),
        compiler_params=_params(("parallel",), 40),
        name="ple",
    )(x, p, g, w_gate, w_proj, g_final)


def _split_hi_lo(w):
    hi = w.astype(BF16)
    lo = (w - hi.astype(F32)).astype(BF16)
    return hi, lo


def _pack_layer(i, g_mix, w_in, b_in, w_conv_dw, b_conv_dw, g_conv_ln, b_conv_ln, w_conv_out,
                w_alpha_f, b_alpha_f, w_alpha_b, b_alpha_b, g_gla, w_gla_out, w_out,
                g_ffn, w_ffn_up, w_ffn_dw, b_ffn_dw, w_ffn_down, g_ple, w_ple_gate, w_ple_proj):
    row = lambda v: v[i].reshape(1, -1).astype(F32)
    z0 = COL_SCONV
    z1 = z0 + 2 * GATE_RANK
    w, b = w_in[i], b_in[i]
    col_scale = jnp.ones((N_MAIN,), F32).at[COL_Q:COL_K].set(GLA_DK ** -0.5)
    w_main = (jnp.concatenate([w[:, :z0], w[:, z1:]], axis=1) * col_scale).astype(BF16)
    b_main = (jnp.concatenate([b[:z0], b[z1:]]) * col_scale).reshape(1, -1)
    wzf, wzb = w[:, z0:z0 + GATE_RANK], w[:, z0 + GATE_RANK:z1]
    used = 6 * GATE_RANK
    pad_w = jnp.zeros((D_MODEL, LANES - used), F32)
    wz = jnp.concatenate([wzf, wzf, wzf, wzb, wzb, wzb, pad_w], axis=1).astype(BF16)
    bzf, bzb = b[z0:z0 + GATE_RANK], b[z0 + GATE_RANK:z1]
    bz = jnp.concatenate([bzf, bzf, bzf, bzb, bzb, bzb, jnp.ones((2,), F32),
                          jnp.zeros((LANES - used - 2,), F32)]).reshape(1, -1)
    fhi, flo = _split_hi_lo(w_alpha_f[i])
    bhi, blo = _split_hi_lo(w_alpha_b[i])
    bias_f = jnp.stack(_split_hi_lo(b_alpha_f[i].astype(F32)))
    bias_b = jnp.stack(_split_hi_lo(b_alpha_b[i].astype(F32)))
    zrows = lambda n: jnp.zeros((n, GLA_KDIM), BF16)
    wa_f = jnp.concatenate([fhi, fhi, flo, zrows(3 * GATE_RANK), bias_f, zrows(LANES - used - 2)], axis=0)
    wa_b = jnp.concatenate([zrows(3 * GATE_RANK), bhi, bhi, blo, bias_b, zrows(LANES - used - 2)], axis=0)
    wdw = jnp.concatenate([w_conv_dw[i], jnp.zeros((1, D_MODEL), F32)], axis=0)
    fdw = jnp.concatenate([w_ffn_dw[i], jnp.zeros((8 - w_ffn_dw.shape[1], FFN_DIM), F32)], axis=0)
    return dict(
        g_mix=row(g_mix), w_main=w_main, b_main=b_main, wz=wz, bz=bz,
        wdw=wdw, bdw=row(b_conv_dw), gln=row(g_conv_ln), bln=row(b_conv_ln),
        w_conv_out=w_conv_out[i].astype(BF16),
        wa_f=wa_f, wa_b=wa_b,
        g_gla=row(g_gla), w_gla_out=w_gla_out[i].astype(BF16), w_out=w_out[i].astype(BF16),
        g_ffn=row(g_ffn), w_ffn_up=w_ffn_up[i].astype(BF16), fdw=fdw, fdb=row(b_ffn_dw),
        w_ffn_down=w_ffn_down[i].astype(BF16),
        g_ple=row(g_ple), w_ple_gate=w_ple_gate[i].astype(BF16), w_ple_proj=w_ple_proj[i].astype(BF16),
    )


def _trunk(x, p, layers, g_final):
    bsz, seq, _ = x.shape
    t = bsz * seq
    x = x.reshape(t, D_MODEL)
    chunk = min(GLA_CHUNK, _tile(seq, GLA_ROWS))
    tri_f, tri_b = _tri(chunk, False), _tri(chunk, True)
    gf = g_final.reshape(1, -1)
    for li, w in enumerate(layers):
        proj, zc = _inproj(x, w["g_mix"], w["w_main"], w["b_main"], w["wz"], w["bz"])
        o, cv_a = _gla_direction(proj, zc, w["wa_f"], tri_f, w["wdw"], w["bdw"], None, seq, False)
        o, cv_b = _gla_direction(proj, zc, w["wa_b"], tri_b, w["wdw"], w["bdw"], o, seq, True)
        x = _merge(o, proj, cv_a, cv_b, x, w["g_gla"], w["gln"], w["bln"],
                   w["w_conv_out"], w["w_gla_out"], w["w_out"])
        x = _ffn(x, seq, w["g_ffn"], w["w_ffn_up"], w["fdw"], w["fdb"], w["w_ffn_down"])
        x = _ple(x, p[li].reshape(t, PLE_DIM), w["g_ple"], w["w_ple_gate"], w["w_ple_proj"], gf,
                 li == len(layers) - 1)
    return x.reshape(bsz, seq, D_MODEL)


def kernel(x_prompt, x_sample, p_prompt, p_sample, g_mix, w_in, b_in, w_conv_dw, b_conv_dw, g_conv_ln,
           b_conv_ln, w_conv_out, w_alpha_f, b_alpha_f, w_alpha_b, b_alpha_b, g_gla, w_gla_out, w_out,
           g_ffn, w_ffn_up, w_ffn_dw, b_ffn_dw, w_ffn_down, g_ple, w_ple_gate, w_ple_proj, g_final):
    layers = [
        _pack_layer(i, g_mix, w_in, b_in, w_conv_dw, b_conv_dw, g_conv_ln, b_conv_ln, w_conv_out,
                    w_alpha_f, b_alpha_f, w_alpha_b, b_alpha_b, g_gla, w_gla_out, w_out,
                    g_ffn, w_ffn_up, w_ffn_dw, b_ffn_dw, w_ffn_down, g_ple, w_ple_gate, w_ple_proj)
        for i in range(g_mix.shape[0])
    ]
    y_prompt = _trunk(x_prompt, p_prompt, layers, g_final)
    y_sample = _trunk(x_sample, p_sample, layers, g_final)
    return (y_prompt, y_sample)
```

```python
import functools

import jax
import jax.numpy as jnp
from jax import lax
from jax.experimental import pallas as pl
from jax.experimental.pallas import tpu as pltpu

F32 = jnp.float32
BF16 = jnp.bfloat16

D_MODEL = 2048
CONV_WIDTH = 31
GLA_HEADS = 4
GLA_DK = 256
GLA_DV = 512
GLA_KDIM = GLA_HEADS * GLA_DK
GATE_RANK = 16
GATE_TAU = 16.0
FFN_DIM = 5632
PLE_DIM = 256
EPS = 1e-6

LANES = 128
HALO = 16
MIB = 1024 * 1024

N_MAIN = 7 * D_MODEL
COL_CVAL, COL_CGATE, COL_Q, COL_K, COL_V, COL_R, COL_SCONV, COL_SGLA = (
    0, 2048, 4096, 5120, 6144, 8192, 10240, 12288)

GLA_CHUNK = 128
GLA_ROWS = 512
CONV_HALF = D_MODEL // 2


def _params(sem, vmem_mib):
    return pltpu.CompilerParams(dimension_semantics=sem, vmem_limit_bytes=vmem_mib * MIB)


def _tile(total, pref):
    t = min(total, pref)
    while total % t:
        t //= 2
    return t


def _rms(x, g):
    ms = jnp.mean(x * x, axis=-1, keepdims=True)
    return x * lax.rsqrt(ms + EPS) * g


def _sigmoid(x):
    return 0.5 * jnp.tanh(0.5 * x) + 0.5


def _const_spec(shape):
    return pl.BlockSpec(shape, lambda *_: (0,) * len(shape), pipeline_mode=pl.Buffered(1))


def _layer_spec(li, shape, index=None, resident=False):
    if index is None:
        index_map = lambda *_: (li,) + (0,) * len(shape)
    else:
        index_map = lambda *g: (li,) + tuple(index(*g))
    return pl.BlockSpec((None,) + tuple(shape), index_map,
                        pipeline_mode=pl.Buffered(1) if resident else None)


def _inproj_kernel(x_ref, g_ref, w_ref, b_ref, wz_ref, bz_ref, o_ref, z_ref, h_ref):
    @pl.when(pl.program_id(1) == 0)
    def _():
        hb = _rms(x_ref[...], g_ref[...]).astype(BF16)
        h_ref[...] = hb
        z = jnp.dot(hb, wz_ref[...], preferred_element_type=F32) + bz_ref[...]
        hi = z.astype(BF16)
        lo = (z - hi.astype(F32)).astype(BF16)
        lane = lax.broadcasted_iota(jnp.int32, z.shape, 1) % (3 * GATE_RANK)
        is_lo = (lane >= GATE_RANK) & (lane < 2 * GATE_RANK)
        z_ref[...] = jnp.where(is_lo, lo, hi)

    acc = jnp.dot(h_ref[...], w_ref[...], preferred_element_type=F32) + b_ref[...]
    o_ref[...] = acc.astype(o_ref.dtype)


def _inproj(x, li, g, w, b, wz, bz):
    t = x.shape[0]
    tm = _tile(t, 1024)
    tn = 2048
    return pl.pallas_call(
        _inproj_kernel,
        grid=(t // tm, N_MAIN // tn),
        in_specs=[
            pl.BlockSpec((tm, D_MODEL), lambda i, j: (i, 0)),
            _layer_spec(li, (1, D_MODEL)),
            _layer_spec(li, (D_MODEL, tn), lambda i, j: (0, j)),
            _layer_spec(li, (1, tn), lambda i, j: (0, j)),
            _layer_spec(li, (D_MODEL, LANES)),
            _layer_spec(li, (1, LANES)),
        ],
        out_specs=[
            pl.BlockSpec((tm, tn), lambda i, j: (i, j)),
            pl.BlockSpec((tm, LANES), lambda i, j: (i, 0)),
        ],
        out_shape=[
            jax.ShapeDtypeStruct((t, N_MAIN), BF16),
            jax.ShapeDtypeStruct((t, LANES), BF16),
        ],
        scratch_shapes=[pltpu.VMEM((tm, D_MODEL), BF16)],
        compiler_params=_params(("parallel", "arbitrary"), 56),
        name="inproj",
    )(x, g, w, b, wz, bz)


def _conv_taps(ext_ref, wdw_ref, r0, lanes, rows):
    span = rows + 8
    acc = None
    for phase in range(8):
        part = None
        for s in range(phase, CONV_WIDTH + 1, 8):
            if s < 1:
                continue
            start = pl.multiple_of(r0 + (s - phase), 8)
            term = ext_ref[pl.ds(start, span), lanes] * wdw_ref[s - 1:s, lanes]
            part = term if part is None else part + term
        if phase:
            part = pltpu.roll(part, span - phase, axis=0)
        part = part[:rows, :]
        acc = part if acc is None else acc + part
    return acc


def _gla_kernel(q_ref, k_ref, v_ref, z_ref, wa_ref, tri_ref,
                cv_ref, cvl_ref, cvr_ref, cg_ref, cgl_ref, cgr_ref, wdw_ref, bdw_ref,
                *rest, reverse, chunk, nchunks, nblk):
    if reverse:
        prev_ref, *rest = rest
    o_ref, cvo_ref, s_ref, ext_ref, qin_ref, qt_ref, kt_ref, kdec_ref, dec_ref = rest
    rows = chunk * nchunks
    blk = pl.program_id(1)
    pos = (nblk - 1 - blk) if reverse else blk

    @pl.when(blk == 0)
    def _():
        s_ref[...] = jnp.zeros_like(s_ref)

    def glu(val_ref, gate_ref):
        return val_ref[...].astype(F32) * _sigmoid(gate_ref[...].astype(F32))

    ext_ref[0:HALO, :] = jnp.where(pos != 0, glu(cvl_ref, cgl_ref), 0.0)
    ext_ref[HALO:HALO + rows, :] = glu(cv_ref, cg_ref)
    ext_ref[HALO + rows:2 * HALO + rows, :] = jnp.where(pos != nblk - 1, glu(cvr_ref, cgr_ref), 0.0)

    half = chunk // 2
    conv_per_head = CONV_HALF // LANES // GLA_HEADS
    row = lax.broadcasted_iota(jnp.int32, (chunk, chunk), 0)
    col = lax.broadcasted_iota(jnp.int32, (chunk, chunk), 1)
    visible = (col >= row) if reverse else (col <= row)
    edge = 0 if reverse else chunk - 1
    mid = half if reverse else half - 1

    for c in range(nchunks):
        rs = slice(c * chunk, (c + 1) * chunk)
        pre = jnp.dot(z_ref[rs, :], wa_ref[...], preferred_element_type=F32)
        la = (jnp.minimum(pre, 0.0) - jnp.log(1.0 + jnp.exp(-jnp.abs(pre)))) * (1.0 / GATE_TAU)
        hi = la.astype(BF16)
        lo = (la - hi.astype(F32)).astype(BF16)
        cum = jnp.dot(tri_ref[...], jnp.concatenate([hi, lo], axis=0), preferred_element_type=F32)
        tot = cum[edge:edge + 1, :]
        ref = cum[mid:mid + 1, :]
        q_in = q_ref[rs, :].astype(F32) * jnp.exp(cum)
        k_t = k_ref[rs, :].astype(F32) * jnp.exp(ref - cum)
        qin_ref[rs, :] = q_in.astype(BF16)
        qt_ref[rs, :] = (q_in * jnp.exp(-ref)).astype(BF16)
        kt_ref[rs, :] = k_t.astype(BF16)
        kdec_ref[rs, :] = (k_t * jnp.exp(tot - ref)).astype(BF16)
        dec_ref[c * 8:(c + 1) * 8, :] = jnp.broadcast_to(jnp.exp(tot), (8, GLA_KDIM))

    def step(ci, carry):
        cidx = (nchunks - 1 - ci) if reverse else ci
        r0 = pl.multiple_of(cidx * chunk, chunk)
        d0 = pl.multiple_of(cidx * 8, 8)
        for h in range(GLA_HEADS):
            kd = slice(h * GLA_DK, (h + 1) * GLA_DK)
            vd = slice(h * GLA_DV, (h + 1) * GLA_DV)
            sc = lax.dot_general(qt_ref[pl.ds(r0, chunk), kd], kt_ref[pl.ds(r0, chunk), kd],
                                 (((1,), (1,)), ((), ())), preferred_element_type=F32)
            sc = jnp.where(visible, sc, 0.0).astype(BF16)
            vh = v_ref[pl.ds(r0, chunk), vd]
            st = s_ref[h]
            o = jnp.dot(sc, vh, preferred_element_type=F32)
            o = o + lax.dot_general(qin_ref[pl.ds(r0, chunk), kd], st.astype(BF16), (((1,), (1,)), ((), ())),
                                    preferred_element_type=F32)
            upd = lax.dot_general(vh, kdec_ref[pl.ds(r0, chunk), kd], (((0,), (0,)), ((), ())),
                                  preferred_element_type=F32)
            s_ref[h] = st * dec_ref[pl.ds(d0, 8), kd][0:1, :] + upd
            if reverse:
                o = o + prev_ref[pl.ds(r0, chunk), vd]
            o_ref[pl.ds(r0, chunk), vd] = o
            for cb in range(h * conv_per_head, (h + 1) * conv_per_head):
                lanes = slice(cb * LANES, (cb + 1) * LANES)
                cvo_ref[pl.ds(r0, chunk), lanes] = (
                    _conv_taps(ext_ref, wdw_ref, r0, lanes, chunk) + bdw_ref[:, lanes])
        return carry

    lax.fori_loop(0, nchunks, step, 0)


def _gla_direction(proj, zc, li, wa, tri, wdw, bdw, prev, seq, reverse):
    t = proj.shape[0]
    rows = _tile(seq, GLA_ROWS)
    chunk = min(GLA_CHUNK, rows)
    nblk = seq // rows
    hb = rows // HALO
    nhalo = t // HALO
    part = 1 if reverse else 0

    def rb(b, c):
        return b * nblk + ((nblk - 1 - c) if reverse else c)

    def conv_specs(col0):
        col = col0 // CONV_HALF + part
        return [
            pl.BlockSpec((rows, CONV_HALF), lambda b, c: (rb(b, c), col)),
            pl.BlockSpec((HALO, CONV_HALF), lambda b, c: (jnp.maximum(rb(b, c) * hb - 1, 0), col)),
            pl.BlockSpec((HALO, CONV_HALF), lambda b, c: (jnp.minimum((rb(b, c) + 1) * hb, nhalo - 1), col)),
        ]

    in_specs = [
        pl.BlockSpec((rows, GLA_KDIM), lambda b, c: (rb(b, c), COL_Q // GLA_KDIM)),
        pl.BlockSpec((rows, GLA_KDIM), lambda b, c: (rb(b, c), COL_K // GLA_KDIM)),
        pl.BlockSpec((rows, D_MODEL), lambda b, c: (rb(b, c), COL_V // D_MODEL)),
        pl.BlockSpec((rows, LANES), lambda b, c: (rb(b, c), 0)),
        _layer_spec(li, (LANES, GLA_KDIM), resident=True),
        _const_spec((chunk, 2 * chunk)),
        *conv_specs(COL_CVAL), *conv_specs(COL_CGATE),
        _layer_spec(li, (CONV_WIDTH + 1, CONV_HALF), lambda b, c: (0, part), resident=True),
        _layer_spec(li, (1, CONV_HALF), lambda b, c: (0, part), resident=True),
    ]
    args = [proj, proj, proj, zc, wa, tri, proj, proj, proj, proj, proj, proj, wdw, bdw]
    if reverse:
        in_specs.append(pl.BlockSpec((rows, D_MODEL), lambda b, c: (rb(b, c), 0)))
        args.append(prev)
    return pl.pallas_call(
        functools.partial(_gla_kernel, reverse=reverse, chunk=chunk, nchunks=rows // chunk, nblk=nblk),
        grid=(t // seq, nblk),
        in_specs=in_specs,
        out_specs=[
            pl.BlockSpec((rows, D_MODEL), lambda b, c: (rb(b, c), 0)),
            pl.BlockSpec((rows, CONV_HALF), lambda b, c: (rb(b, c), 0)),
        ],
        out_shape=[
            jax.ShapeDtypeStruct((t, D_MODEL), F32),
            jax.ShapeDtypeStruct((t, CONV_HALF), F32),
        ],
        scratch_shapes=[
            pltpu.VMEM((GLA_HEADS, GLA_DV, GLA_DK), F32),
            pltpu.VMEM((rows + 2 * HALO, CONV_HALF), F32),
            pltpu.VMEM((rows, GLA_KDIM), BF16),
            pltpu.VMEM((rows, GLA_KDIM), BF16),
            pltpu.VMEM((rows, GLA_KDIM), BF16),
            pltpu.VMEM((rows, GLA_KDIM), BF16),
            pltpu.VMEM((8 * (rows // chunk), GLA_KDIM), F32),
        ],
        compiler_params=_params(("parallel", "arbitrary"), 56),
        name="gla_bwd" if reverse else "gla_fwd",
    )(*args)


def _tri(chunk, reverse):
    r = jnp.arange(chunk)[:, None]
    c = jnp.arange(chunk)[None, :]
    m = ((c >= r) if reverse else (c <= r)).astype(BF16)
    return jnp.concatenate([m, m], axis=1)


def _merge_kernel(o_ref, r_ref, sc_ref, sg_ref, cva_ref, cvb_ref, x_ref, gg_ref, gln_ref, bln_ref,
                  wc_ref, wg_ref, wo_ref, out_ref, on_ref, act_ref):
    for h in range(GLA_HEADS):
        vd = slice(h * GLA_DV, (h + 1) * GLA_DV)
        r = r_ref[:, vd].astype(F32)
        on = _rms(o_ref[:, vd], gg_ref[:, vd]) * (r * _sigmoid(r))
        on_ref[:, vd] = on.astype(BF16)
    y_gla = jnp.dot(on_ref[...], wg_ref[...], preferred_element_type=F32)

    ca, cb = cva_ref[...], cvb_ref[...]
    lo, hi = slice(0, CONV_HALF), slice(CONV_HALF, D_MODEL)
    mean = (jnp.sum(ca, axis=-1, keepdims=True) + jnp.sum(cb, axis=-1, keepdims=True)) * (1.0 / D_MODEL)
    ca, cb = ca - mean, cb - mean
    var = (jnp.sum(ca * ca, axis=-1, keepdims=True) + jnp.sum(cb * cb, axis=-1, keepdims=True)) * (1.0 / D_MODEL)
    inv = lax.rsqrt(var + EPS)
    for part, sl in ((ca, lo), (cb, hi)):
        ln = part * inv * gln_ref[:, sl] + bln_ref[:, sl]
        act_ref[:, sl] = (ln * _sigmoid(ln)).astype(BF16)
    y_conv = jnp.dot(act_ref[...], wc_ref[...], preferred_element_type=F32)

    mixed = (_sigmoid(sc_ref[...].astype(F32)) * y_conv
             + _sigmoid(sg_ref[...].astype(F32)) * y_gla)
    out_ref[...] = x_ref[...] + jnp.dot(mixed.astype(BF16), wo_ref[...], preferred_element_type=F32)


def _merge(o, proj, cv_a, cv_b, x, li, g_gla, gln, bln, w_conv_out, w_gla_out, w_out):
    t = x.shape[0]
    tm = _tile(t, 256)
    blk = lambda col: pl.BlockSpec((tm, D_MODEL), lambda i: (i, col))
    half = pl.BlockSpec((tm, CONV_HALF), lambda i: (i, 0))
    return pl.pallas_call(
        _merge_kernel,
        grid=(t // tm,),
        in_specs=[
            blk(0), blk(COL_R // D_MODEL), blk(COL_SCONV // D_MODEL), blk(COL_SGLA // D_MODEL),
            half, half, blk(0),
            *[_layer_spec(li, (1, D_MODEL), resident=True) for _ in range(3)],
            *[_layer_spec(li, (D_MODEL, D_MODEL), resident=True) for _ in range(3)],
        ],
        out_specs=blk(0),
        out_shape=jax.ShapeDtypeStruct((t, D_MODEL), F32),
        scratch_shapes=[pltpu.VMEM((tm, D_MODEL), BF16), pltpu.VMEM((tm, D_MODEL), BF16)],
        compiler_params=_params(("parallel",), 56),
        name="merge",
    )(o, proj, proj, proj, cv_a, cv_b, x, g_gla, gln, bln, w_conv_out, w_gla_out, w_out)


def _gelu_tanh(x):
    return 0.5 * x * (1.0 + jnp.tanh(0.7978845608028654 * (x + 0.044715 * (x * x * x))))


def _ffn_kernel(x_ref, xl_ref, xr_ref, g_ref, wg_ref, wv_ref, wdw_ref, bdw_ref, wd_ref, out_ref, h_ref,
                *, tiles_per_seq, tm):
    f = pl.program_id(1)

    @pl.when(f == 0)
    def _():
        t = pl.program_id(0) % tiles_per_seq
        g = g_ref[...]
        x = x_ref[...]
        h_ref[0:HALO, :] = jnp.where(t != 0, _rms(xl_ref[...], g), 0.0).astype(BF16)
        h_ref[HALO:HALO + tm, :] = _rms(x, g).astype(BF16)
        h_ref[HALO + tm:2 * HALO + tm, :] = jnp.where(
            t != tiles_per_seq - 1, _rms(xr_ref[...], g), 0.0).astype(BF16)
        out_ref[...] = x

    gt = jnp.dot(h_ref[...], wg_ref[...], preferred_element_type=F32)
    val = jnp.dot(h_ref[HALO:HALO + tm, :], wv_ref[...], preferred_element_type=F32)
    conv = (gt[HALO - 1:HALO - 1 + tm, :] * wdw_ref[0:1, :]
            + gt[HALO:HALO + tm, :] * wdw_ref[1:2, :]
            + gt[HALO + 1:HALO + 1 + tm, :] * wdw_ref[2:3, :]
            + bdw_ref[...])
    act = (_gelu_tanh(conv) * val).astype(BF16)
    out_ref[...] += jnp.dot(act, wd_ref[...], preferred_element_type=F32)


def _ffn(x, seq, li, g, w_up, wdw, bdw, w_down):
    t = x.shape[0]
    tm = _tile(seq, 1024)
    tf = 512
    nf = FFN_DIM // tf
    hb = tm // HALO
    nhalo = t // HALO
    return pl.pallas_call(
        functools.partial(_ffn_kernel, tiles_per_seq=seq // tm, tm=tm),
        grid=(t // tm, nf),
        in_specs=[
            pl.BlockSpec((tm, D_MODEL), lambda i, f: (i, 0)),
            pl.BlockSpec((HALO, D_MODEL), lambda i, f: (jnp.maximum(i * hb - 1, 0), 0)),
            pl.BlockSpec((HALO, D_MODEL), lambda i, f: (jnp.minimum((i + 1) * hb, nhalo - 1), 0)),
            _layer_spec(li, (1, D_MODEL)),
            _layer_spec(li, (D_MODEL, tf), lambda i, f: (0, f)),
            _layer_spec(li, (D_MODEL, tf), lambda i, f: (0, f + nf)),
            _layer_spec(li, (8, tf), lambda i, f: (0, f)),
            _layer_spec(li, (1, tf), lambda i, f: (0, f)),
            _layer_spec(li, (tf, D_MODEL), lambda i, f: (f, 0)),
        ],
        out_specs=pl.BlockSpec((tm, D_MODEL), lambda i, f: (i, 0)),
        out_shape=jax.ShapeDtypeStruct((t, D_MODEL), F32),
        scratch_shapes=[pltpu.VMEM((tm + 2 * HALO, D_MODEL), BF16)],
        compiler_params=_params(("parallel", "arbitrary"), 56),
        name="ffn",
    )(x, x, x, g, w_up, w_up, wdw, bdw, w_down)


def _ple_kernel(x_ref, p_ref, g_ref, wg_ref, wp_ref, gf_ref, out_ref, *, final):
    x = x_ref[...]
    gate = _sigmoid(jnp.dot(_rms(x, g_ref[...]).astype(BF16), wg_ref[...], preferred_element_type=F32))
    y = x + gate * jnp.dot(p_ref[...].astype(BF16), wp_ref[...], preferred_element_type=F32)
    if final:
        y = _rms(y, gf_ref[...])
    out_ref[...] = y


def _ple(x, p, li, g, w_gate, w_proj, g_final, final):
    t = x.shape[0]
    tm = _tile(t, 512)
    return pl.pallas_call(
        functools.partial(_ple_kernel, final=final),
        grid=(t // tm,),
        in_specs=[
            pl.BlockSpec((tm, D_MODEL), lambda i: (i, 0)),
            _layer_spec(li, (tm, PLE_DIM), lambda i: (i, 0)),
            _layer_spec(li, (1, D_MODEL), resident=True),
            _layer_spec(li, (D_MODEL, D_MODEL), resident=True),
            _layer_spec(li, (PLE_DIM, D_MODEL), resident=True),
            _const_spec((1, D_MODEL)),
        ],
        out_specs=pl.BlockSpec((tm, D_MODEL), lambda i: (i, 0)),
        out_shape=jax.ShapeDtypeStruct((t, D_MODEL), F32),
        compiler_params=_params(("parallel",), 40),
        name="ple",
    )(x, p, g, w_gate, w_proj, g_final)


def _split_hi_lo(w):
    hi = w.astype(BF16)
    lo = (w - hi.astype(F32)).astype(BF16)
    return hi, lo


def _pack_params(g_mix, w_in, b_in, w_conv_dw, b_conv_dw, g_conv_ln, b_conv_ln, w_conv_out,
                 w_alpha_f, b_alpha_f, w_alpha_b, b_alpha_b, g_gla, w_gla_out, w_out,
                 g_ffn, w_ffn_up, w_ffn_dw, b_ffn_dw, w_ffn_down, g_ple, w_ple_gate, w_ple_proj):
    nl = g_mix.shape[0]
    rows = lambda v: v.reshape(nl, 1, -1).astype(F32)
    z0 = COL_SCONV
    z1 = z0 + 2 * GATE_RANK
    qs = GLA_DK ** -0.5
    w_main = jnp.concatenate(
        [w_in[:, :, :COL_Q], w_in[:, :, COL_Q:COL_K] * qs, w_in[:, :, COL_K:z0], w_in[:, :, z1:]], axis=2).astype(BF16)
    b_main = jnp.concatenate(
        [b_in[:, :COL_Q], b_in[:, COL_Q:COL_K] * qs, b_in[:, COL_K:z0], b_in[:, z1:]], axis=1).reshape(nl, 1, -1)
    wzf, wzb = w_in[:, :, z0:z0 + GATE_RANK], w_in[:, :, z0 + GATE_RANK:z1]
    used = 6 * GATE_RANK
    pad_w = jnp.zeros((nl, D_MODEL, LANES - used), F32)
    wz = jnp.concatenate([wzf, wzf, wzf, wzb, wzb, wzb, pad_w], axis=2).astype(BF16)
    bzf, bzb = b_in[:, z0:z0 + GATE_RANK], b_in[:, z0 + GATE_RANK:z1]
    bz = jnp.concatenate([bzf, bzf, bzf, bzb, bzb, bzb, jnp.ones((nl, 2), F32),
                          jnp.zeros((nl, LANES - used - 2), F32)], axis=1).reshape(nl, 1, -1)
    fhi, flo = _split_hi_lo(w_alpha_f)
    bhi, blo = _split_hi_lo(w_alpha_b)
    bias_f = jnp.stack(_split_hi_lo(b_alpha_f.astype(F32)), axis=1)
    bias_b = jnp.stack(_split_hi_lo(b_alpha_b.astype(F32)), axis=1)
    zrows = lambda n: jnp.zeros((nl, n, GLA_KDIM), BF16)
    wa_f = jnp.concatenate([fhi, fhi, flo, zrows(3 * GATE_RANK), bias_f, zrows(LANES - used - 2)], axis=1)
    wa_b = jnp.concatenate([zrows(3 * GATE_RANK), bhi, bhi, blo, bias_b, zrows(LANES - used - 2)], axis=1)
    wdw = jnp.concatenate([w_conv_dw, jnp.zeros((nl, 1, D_MODEL), F32)], axis=1)
    fdw = jnp.concatenate([w_ffn_dw, jnp.zeros((nl, 8 - w_ffn_dw.shape[1], FFN_DIM), F32)], axis=1)
    return dict(
        g_mix=rows(g_mix), w_main=w_main, b_main=b_main, wz=wz, bz=bz,
        wdw=wdw, bdw=rows(b_conv_dw), gln=rows(g_conv_ln), bln=rows(b_conv_ln),
        w_conv_out=w_conv_out.astype(BF16),
        wa_f=wa_f, wa_b=wa_b,
        g_gla=rows(g_gla), w_gla_out=w_gla_out.astype(BF16), w_out=w_out.astype(BF16),
        g_ffn=rows(g_ffn), w_ffn_up=w_ffn_up.astype(BF16), fdw=fdw, fdb=rows(b_ffn_dw),
        w_ffn_down=w_ffn_down.astype(BF16),
        g_ple=rows(g_ple), w_ple_gate=w_ple_gate.astype(BF16), w_ple_proj=w_ple_proj.astype(BF16),
    )


def _trunk(x, p, w, g_final):
    bsz, seq, _ = x.shape
    t = bsz * seq
    nl = p.shape[0]
    x = x.reshape(t, D_MODEL)
    p = p.reshape(nl, t, PLE_DIM)
    chunk = min(GLA_CHUNK, _tile(seq, GLA_ROWS))
    tri_f, tri_b = _tri(chunk, False), _tri(chunk, True)
    gf = g_final.reshape(1, -1)
    for li in range(nl):
        proj, zc = _inproj(x, li, w["g_mix"], w["w_main"], w["b_main"], w["wz"], w["bz"])
        o, cv_a = _gla_direction(proj, zc, li, w["wa_f"], tri_f, w["wdw"], w["bdw"], None, seq, False)
        o, cv_b = _gla_direction(proj, zc, li, w["wa_b"], tri_b, w["wdw"], w["bdw"], o, seq, True)
        x = _merge(o, proj, cv_a, cv_b, x, li, w["g_gla"], w["gln"], w["bln"],
                   w["w_conv_out"], w["w_gla_out"], w["w_out"])
        x = _ffn(x, seq, li, w["g_ffn"], w["w_ffn_up"], w["fdw"], w["fdb"], w["w_ffn_down"])
        x = _ple(x, p, li, w["g_ple"], w["w_ple_gate"], w["w_ple_proj"], gf, li == nl - 1)
    return x.reshape(bsz, seq, D_MODEL)


def kernel(x_prompt, x_sample, p_prompt, p_sample, g_mix, w_in, b_in, w_conv_dw, b_conv_dw, g_conv_ln,
           b_conv_ln, w_conv_out, w_alpha_f, b_alpha_f, w_alpha_b, b_alpha_b, g_gla, w_gla_out, w_out,
           g_ffn, w_ffn_up, w_ffn_dw, b_ffn_dw, w_ffn_down, g_ple, w_ple_gate, w_ple_proj, g_final):
    params = _pack_params(g_mix, w_in, b_in, w_conv_dw, b_conv_dw, g_conv_ln, b_conv_ln, w_conv_out,
                          w_alpha_f, b_alpha_f, w_alpha_b, b_alpha_b, g_gla, w_gla_out, w_out,
                          g_ffn, w_ffn_up, w_ffn_dw, b_ffn_dw, w_ffn_down, g_ple, w_ple_gate, w_ple_proj)
    y_prompt = _trunk(x_prompt, p_prompt, params, g_final)
    y_sample = _trunk(x_sample, p_sample, params, g_final)
    return (y_prompt, y_sample)
```

```python
import functools

import jax
import jax.numpy as jnp
from jax import lax
from jax.experimental import pallas as pl
from jax.experimental.pallas import tpu as pltpu

F32 = jnp.float32
BF16 = jnp.bfloat16

D_MODEL = 2048
CONV_WIDTH = 31
GLA_HEADS = 4
GLA_DK = 256
GLA_DV = 512
GLA_KDIM = GLA_HEADS * GLA_DK
GATE_RANK = 16
GATE_TAU = 16.0
FFN_DIM = 5632
PLE_DIM = 256
EPS = 1e-6

LANES = 128
HALO = 16
MIB = 1024 * 1024

N_MAIN = 7 * D_MODEL
COL_CVAL, COL_CGATE, COL_Q, COL_K, COL_V, COL_R, COL_SCONV, COL_SGLA = (
    0, 2048, 4096, 5120, 6144, 8192, 10240, 12288)

GLA_CHUNK = 128
GLA_ROWS = 512
CONV_HALF = D_MODEL // 2
CONV_SPAN = 256


def _params(sem, vmem_mib):
    return pltpu.CompilerParams(dimension_semantics=sem, vmem_limit_bytes=vmem_mib * MIB)


def _tile(total, pref):
    t = min(total, pref)
    while total % t:
        t //= 2
    return t


def _rms(x, g):
    ms = jnp.mean(x * x, axis=-1, keepdims=True)
    return x * lax.rsqrt(ms + EPS) * g


def _sigmoid(x):
    return 0.5 * jnp.tanh(0.5 * x) + 0.5


def _const_spec(shape):
    return pl.BlockSpec(shape, lambda *_: (0,) * len(shape), pipeline_mode=pl.Buffered(1))


def _layer_spec(li, shape, index=None, resident=False):
    if index is None:
        index_map = lambda *_: (li,) + (0,) * len(shape)
    else:
        index_map = lambda *g: (li,) + tuple(index(*g))
    return pl.BlockSpec((None,) + tuple(shape), index_map,
                        pipeline_mode=pl.Buffered(1) if resident else None)


def _inproj_kernel(x_ref, g_ref, w_ref, b_ref, wz_ref, bz_ref, o_ref, z_ref, h_ref):
    @pl.when(pl.program_id(1) == 0)
    def _():
        hb = _rms(x_ref[...], g_ref[...]).astype(BF16)
        h_ref[...] = hb
        z = jnp.dot(hb, wz_ref[...], preferred_element_type=F32) + bz_ref[...]
        hi = z.astype(BF16)
        lo = (z - hi.astype(F32)).astype(BF16)
        lane = lax.broadcasted_iota(jnp.int32, z.shape, 1) % (3 * GATE_RANK)
        is_lo = (lane >= GATE_RANK) & (lane < 2 * GATE_RANK)
        z_ref[...] = jnp.where(is_lo, lo, hi)

    acc = jnp.dot(h_ref[...], w_ref[...], preferred_element_type=F32) + b_ref[...]
    o_ref[...] = acc.astype(o_ref.dtype)


def _inproj(x, li, g, w, b, wz, bz):
    t = x.shape[0]
    tm = _tile(t, 1024)
    tn = 2048
    return pl.pallas_call(
        _inproj_kernel,
        grid=(t // tm, N_MAIN // tn),
        in_specs=[
            pl.BlockSpec((tm, D_MODEL), lambda i, j: (i, 0)),
            _layer_spec(li, (1, D_MODEL)),
            _layer_spec(li, (D_MODEL, tn), lambda i, j: (0, j)),
            _layer_spec(li, (1, tn), lambda i, j: (0, j)),
            _layer_spec(li, (D_MODEL, LANES)),
            _layer_spec(li, (1, LANES)),
        ],
        out_specs=[
            pl.BlockSpec((tm, tn), lambda i, j: (i, j)),
            pl.BlockSpec((tm, LANES), lambda i, j: (i, 0)),
        ],
        out_shape=[
            jax.ShapeDtypeStruct((t, N_MAIN), BF16),
            jax.ShapeDtypeStruct((t, LANES), BF16),
        ],
        scratch_shapes=[pltpu.VMEM((tm, D_MODEL), BF16)],
        compiler_params=_params(("parallel", "arbitrary"), 56),
        name="inproj",
    )(x, g, w, b, wz, bz)


def _conv_taps(ext_ref, wdw_ref, r0, lanes, rows):
    span = rows + 8
    acc = None
    for phase in range(8):
        part = None
        for s in range(phase, CONV_WIDTH + 1, 8):
            if s < 1:
                continue
            start = r0 + (s - phase)
            term = ext_ref[start:start + span, lanes] * wdw_ref[s - 1:s, lanes]
            part = term if part is None else part + term
        if phase:
            part = pltpu.roll(part, span - phase, axis=0)
        part = part[:rows, :]
        acc = part if acc is None else acc + part
    return acc


def _gla_kernel(q_ref, k_ref, v_ref, z_ref, wa_ref, tri_ref,
                cv_ref, cvl_ref, cvr_ref, cg_ref, cgl_ref, cgr_ref, wdw_ref, bdw_ref,
                *rest, reverse, chunk, nchunks, nblk):
    if reverse:
        prev_ref, *rest = rest
    o_ref, cvo_ref, s_ref, ext_ref, qin_ref, qt_ref, kt_ref, kdec_ref, dec_ref = rest
    rows = chunk * nchunks
    blk = pl.program_id(1)
    pos = (nblk - 1 - blk) if reverse else blk

    @pl.when(blk == 0)
    def _():
        s_ref[...] = jnp.zeros_like(s_ref)

    def glu(val_ref, gate_ref):
        return val_ref[...].astype(F32) * _sigmoid(gate_ref[...].astype(F32))

    ext_ref[0:HALO, :] = jnp.where(pos != 0, glu(cvl_ref, cgl_ref), 0.0)
    ext_ref[HALO:HALO + rows, :] = glu(cv_ref, cg_ref)
    ext_ref[HALO + rows:2 * HALO + rows, :] = jnp.where(pos != nblk - 1, glu(cvr_ref, cgr_ref), 0.0)

    half = chunk // 2
    row = lax.broadcasted_iota(jnp.int32, (chunk, chunk), 0)
    col = lax.broadcasted_iota(jnp.int32, (chunk, chunk), 1)
    visible = (col >= row) if reverse else (col <= row)
    edge = 0 if reverse else chunk - 1
    mid = half if reverse else half - 1

    for c in range(nchunks):
        rs = slice(c * chunk, (c + 1) * chunk)
        pre = jnp.dot(z_ref[rs, :], wa_ref[...], preferred_element_type=F32)
        la = (jnp.minimum(pre, 0.0) - jnp.log(1.0 + jnp.exp(-jnp.abs(pre)))) * (1.0 / GATE_TAU)
        hi = la.astype(BF16)
        lo = (la - hi.astype(F32)).astype(BF16)
        cum = jnp.dot(tri_ref[...], jnp.concatenate([hi, lo], axis=0), preferred_element_type=F32)
        tot = cum[edge:edge + 1, :]
        ref = cum[mid:mid + 1, :]
        q_in = q_ref[rs, :].astype(F32) * jnp.exp(cum)
        k_t = k_ref[rs, :].astype(F32) * jnp.exp(ref - cum)
        qin_ref[rs, :] = q_in.astype(BF16)
        qt_ref[rs, :] = (q_in * jnp.exp(-ref)).astype(BF16)
        kt_ref[rs, :] = k_t.astype(BF16)
        kdec_ref[rs, :] = (k_t * jnp.exp(tot - ref)).astype(BF16)
        dec_ref[c * 8:(c + 1) * 8, :] = jnp.broadcast_to(jnp.exp(tot), (8, GLA_KDIM))

    span = min(CONV_SPAN, rows)
    for cb in range(CONV_HALF // LANES):
        lanes = slice(cb * LANES, (cb + 1) * LANES)
        for r0 in range(0, rows, span):
            cvo_ref[r0:r0 + span, lanes] = _conv_taps(ext_ref, wdw_ref, r0, lanes, span) + bdw_ref[:, lanes]

    def step(ci, carry):
        cidx = (nchunks - 1 - ci) if reverse else ci
        r0 = pl.multiple_of(cidx * chunk, chunk)
        d0 = pl.multiple_of(cidx * 8, 8)
        for h in range(GLA_HEADS):
            kd = slice(h * GLA_DK, (h + 1) * GLA_DK)
            vd = slice(h * GLA_DV, (h + 1) * GLA_DV)
            sc = lax.dot_general(qt_ref[pl.ds(r0, chunk), kd], kt_ref[pl.ds(r0, chunk), kd],
                                 (((1,), (1,)), ((), ())), preferred_element_type=F32)
            sc = jnp.where(visible, sc, 0.0).astype(BF16)
            vh = v_ref[pl.ds(r0, chunk), vd]
            st = s_ref[h]
            o = jnp.dot(sc, vh, preferred_element_type=F32)
            o = o + lax.dot_general(qin_ref[pl.ds(r0, chunk), kd], st.astype(BF16), (((1,), (1,)), ((), ())),
                                    preferred_element_type=F32)
            upd = lax.dot_general(vh, kdec_ref[pl.ds(r0, chunk), kd], (((0,), (0,)), ((), ())),
                                  preferred_element_type=F32)
            s_ref[h] = st * dec_ref[pl.ds(d0, 8), kd][0:1, :] + upd
            if reverse:
                o = o + prev_ref[pl.ds(r0, chunk), vd]
            o_ref[pl.ds(r0, chunk), vd] = o
        return carry

    lax.fori_loop(0, nchunks, step, 0)


def _gla_direction(proj, zc, li, wa, tri, wdw, bdw, prev, seq, reverse):
    t = proj.shape[0]
    rows = _tile(seq, GLA_ROWS)
    chunk = min(GLA_CHUNK, rows)
    nblk = seq // rows
    hb = rows // HALO
    nhalo = t // HALO
    part = 1 if reverse else 0

    def rb(b, c):
        return b * nblk + ((nblk - 1 - c) if reverse else c)

    def conv_specs(col0):
        col = col0 // CONV_HALF + part
        return [
            pl.BlockSpec((rows, CONV_HALF), lambda b, c: (rb(b, c), col)),
            pl.BlockSpec((HALO, CONV_HALF), lambda b, c: (jnp.maximum(rb(b, c) * hb - 1, 0), col)),
            pl.BlockSpec((HALO, CONV_HALF), lambda b, c: (jnp.minimum((rb(b, c) + 1) * hb, nhalo - 1), col)),
        ]

    in_specs = [
        pl.BlockSpec((rows, GLA_KDIM), lambda b, c: (rb(b, c), COL_Q // GLA_KDIM)),
        pl.BlockSpec((rows, GLA_KDIM), lambda b, c: (rb(b, c), COL_K // GLA_KDIM)),
        pl.BlockSpec((rows, D_MODEL), lambda b, c: (rb(b, c), COL_V // D_MODEL)),
        pl.BlockSpec((rows, LANES), lambda b, c: (rb(b, c), 0)),
        _layer_spec(li, (LANES, GLA_KDIM), resident=True),
        _const_spec((chunk, 2 * chunk)),
        *conv_specs(COL_CVAL), *conv_specs(COL_CGATE),
        _layer_spec(li, (CONV_WIDTH + 1, CONV_HALF), lambda b, c: (0, part), resident=True),
        _layer_spec(li, (1, CONV_HALF), lambda b, c: (0, part), resident=True),
    ]
    args = [proj, proj, proj, zc, wa, tri, proj, proj, proj, proj, proj, proj, wdw, bdw]
    if reverse:
        in_specs.append(pl.BlockSpec((rows, D_MODEL), lambda b, c: (rb(b, c), 0)))
        args.append(prev)
    return pl.pallas_call(
        functools.partial(_gla_kernel, reverse=reverse, chunk=chunk, nchunks=rows // chunk, nblk=nblk),
        grid=(t // seq, nblk),
        in_specs=in_specs,
        out_specs=[
            pl.BlockSpec((rows, D_MODEL), lambda b, c: (rb(b, c), 0)),
            pl.BlockSpec((rows, CONV_HALF), lambda b, c: (rb(b, c), 0)),
        ],
        out_shape=[
            jax.ShapeDtypeStruct((t, D_MODEL), F32),
            jax.ShapeDtypeStruct((t, CONV_HALF), F32),
        ],
        scratch_shapes=[
            pltpu.VMEM((GLA_HEADS, GLA_DV, GLA_DK), F32),
            pltpu.VMEM((rows + 2 * HALO, CONV_HALF), F32),
            pltpu.VMEM((rows, GLA_KDIM), BF16),
            pltpu.VMEM((rows, GLA_KDIM), BF16),
            pltpu.VMEM((rows, GLA_KDIM), BF16),
            pltpu.VMEM((rows, GLA_KDIM), BF16),
            pltpu.VMEM((8 * (rows // chunk), GLA_KDIM), F32),
        ],
        compiler_params=_params(("parallel", "arbitrary"), 56),
        name="gla_bwd" if reverse else "gla_fwd",
    )(*args)


def _tri(chunk, reverse):
    r = jnp.arange(chunk)[:, None]
    c = jnp.arange(chunk)[None, :]
    m = ((c >= r) if reverse else (c <= r)).astype(BF16)
    return jnp.concatenate([m, m], axis=1)


def _merge_kernel(o_ref, r_ref, sc_ref, sg_ref, cva_ref, cvb_ref, x_ref, gg_ref, gln_ref, bln_ref,
                  wc_ref, wg_ref, wo_ref, out_ref, on_ref, act_ref):
    for h in range(GLA_HEADS):
        vd = slice(h * GLA_DV, (h + 1) * GLA_DV)
        r = r_ref[:, vd].astype(F32)
        on = _rms(o_ref[:, vd], gg_ref[:, vd]) * (r * _sigmoid(r))
        on_ref[:, vd] = on.astype(BF16)
    y_gla = jnp.dot(on_ref[...], wg_ref[...], preferred_element_type=F32)

    ca, cb = cva_ref[...], cvb_ref[...]
    lo, hi = slice(0, CONV_HALF), slice(CONV_HALF, D_MODEL)
    mean = (jnp.sum(ca, axis=-1, keepdims=True) + jnp.sum(cb, axis=-1, keepdims=True)) * (1.0 / D_MODEL)
    ca, cb = ca - mean, cb - mean
    var = (jnp.sum(ca * ca, axis=-1, keepdims=True) + jnp.sum(cb * cb, axis=-1, keepdims=True)) * (1.0 / D_MODEL)
    inv = lax.rsqrt(var + EPS)
    for part, sl in ((ca, lo), (cb, hi)):
        ln = part * inv * gln_ref[:, sl] + bln_ref[:, sl]
        act_ref[:, sl] = (ln * _sigmoid(ln)).astype(BF16)
    y_conv = jnp.dot(act_ref[...], wc_ref[...], preferred_element_type=F32)

    mixed = (_sigmoid(sc_ref[...].astype(F32)) * y_conv
             + _sigmoid(sg_ref[...].astype(F32)) * y_gla)
    out_ref[...] = x_ref[...] + jnp.dot(mixed.astype(BF16), wo_ref[...], preferred_element_type=F32)


def _merge(o, proj, cv_a, cv_b, x, li, g_gla, gln, bln, w_conv_out, w_gla_out, w_out):
    t = x.shape[0]
    tm = _tile(t, 256)
    blk = lambda col: pl.BlockSpec((tm, D_MODEL), lambda i: (i, col))
    half = pl.BlockSpec((tm, CONV_HALF), lambda i: (i, 0))
    return pl.pallas_call(
        _merge_kernel,
        grid=(t // tm,),
        in_specs=[
            blk(0), blk(COL_R // D_MODEL), blk(COL_SCONV // D_MODEL), blk(COL_SGLA // D_MODEL),
            half, half, blk(0),
            *[_layer_spec(li, (1, D_MODEL), resident=True) for _ in range(3)],
            *[_layer_spec(li, (D_MODEL, D_MODEL), resident=True) for _ in range(3)],
        ],
        out_specs=blk(0),
        out_shape=jax.ShapeDtypeStruct((t, D_MODEL), F32),
        scratch_shapes=[pltpu.VMEM((tm, D_MODEL), BF16), pltpu.VMEM((tm, D_MODEL), BF16)],
        compiler_params=_params(("parallel",), 56),
        name="merge",
    )(o, proj, proj, proj, cv_a, cv_b, x, g_gla, gln, bln, w_conv_out, w_gla_out, w_out)


def _gelu_tanh(x):
    return 0.5 * x * (1.0 + jnp.tanh(0.7978845608028654 * (x + 0.044715 * (x * x * x))))


def _ffn_kernel(x_ref, xl_ref, xr_ref, g_ref, wg_ref, wv_ref, wdw_ref, bdw_ref, wd_ref, out_ref, h_ref,
                *, tiles_per_seq, tm):
    f = pl.program_id(1)

    @pl.when(f == 0)
    def _():
        t = pl.program_id(0) % tiles_per_seq
        g = g_ref[...]
        x = x_ref[...]
        h_ref[0:HALO, :] = jnp.where(t != 0, _rms(xl_ref[...], g), 0.0).astype(BF16)
        h_ref[HALO:HALO + tm, :] = _rms(x, g).astype(BF16)
        h_ref[HALO + tm:2 * HALO + tm, :] = jnp.where(
            t != tiles_per_seq - 1, _rms(xr_ref[...], g), 0.0).astype(BF16)
        out_ref[...] = x

    gt = jnp.dot(h_ref[...], wg_ref[...], preferred_element_type=F32)
    val = jnp.dot(h_ref[HALO:HALO + tm, :], wv_ref[...], preferred_element_type=F32)
    conv = (gt[HALO - 1:HALO - 1 + tm, :] * wdw_ref[0:1, :]
            + gt[HALO:HALO + tm, :] * wdw_ref[1:2, :]
            + gt[HALO + 1:HALO + 1 + tm, :] * wdw_ref[2:3, :]
            + bdw_ref[...])
    act = (_gelu_tanh(conv) * val).astype(BF16)
    out_ref[...] += jnp.dot(act, wd_ref[...], preferred_element_type=F32)


def _ffn(x, seq, li, g, w_up, wdw, bdw, w_down):
    t = x.shape[0]
    tm = _tile(seq, 1024)
    tf = 512
    nf = FFN_DIM // tf
    hb = tm // HALO
    nhalo = t // HALO
    return pl.pallas_call(
        functools.partial(_ffn_kernel, tiles_per_seq=seq // tm, tm=tm),
        grid=(t // tm, nf),
        in_specs=[
            pl.BlockSpec((tm, D_MODEL), lambda i, f: (i, 0)),
            pl.BlockSpec((HALO, D_MODEL), lambda i, f: (jnp.maximum(i * hb - 1, 0), 0)),
            pl.BlockSpec((HALO, D_MODEL), lambda i, f: (jnp.minimum((i + 1) * hb, nhalo - 1), 0)),
            _layer_spec(li, (1, D_MODEL)),
            _layer_spec(li, (D_MODEL, tf), lambda i, f: (0, f)),
            _layer_spec(li, (D_MODEL, tf), lambda i, f: (0, f + nf)),
            _layer_spec(li, (8, tf), lambda i, f: (0, f)),
            _layer_spec(li, (1, tf), lambda i, f: (0, f)),
            _layer_spec(li, (tf, D_MODEL), lambda i, f: (f, 0)),
        ],
        out_specs=pl.BlockSpec((tm, D_MODEL), lambda i, f: (i, 0)),
        out_shape=jax.ShapeDtypeStruct((t, D_MODEL), F32),
        scratch_shapes=[pltpu.VMEM((tm + 2 * HALO, D_MODEL), BF16)],
        compiler_params=_params(("parallel", "arbitrary"), 56),
        name="ffn",
    )(x, x, x, g, w_up, w_up, wdw, bdw, w_down)


def _ple_kernel(x_ref, p_ref, g_ref, wg_ref, wp_ref, gf_ref, out_ref, *, final):
    x = x_ref[...]
    gate = _sigmoid(jnp.dot(_rms(x, g_ref[...]).astype(BF16), wg_ref[...], preferred_element_type=F32))
    y = x + gate * jnp.dot(p_ref[...].astype(BF16), wp_ref[...], preferred_element_type=F32)
    if final:
        y = _rms(y, gf_ref[...])
    out_ref[...] = y


def _ple(x, p, li, g, w_gate, w_proj, g_final, final):
    t = x.shape[0]
    tm = _tile(t, 512)
    return pl.pallas_call(
        functools.partial(_ple_kernel, final=final),
        grid=(t // tm,),
        in_specs=[
            pl.BlockSpec((tm, D_MODEL), lambda i: (i, 0)),
            _layer_spec(li, (tm, PLE_DIM), lambda i: (i, 0)),
            _layer_spec(li, (1, D_MODEL), resident=True),
            _layer_spec(li, (D_MODEL, D_MODEL), resident=True),
            _layer_spec(li, (PLE_DIM, D_MODEL), resident=True),
            _const_spec((1, D_MODEL)),
        ],
        out_specs=pl.BlockSpec((tm, D_MODEL), lambda i: (i, 0)),
        out_shape=jax.ShapeDtypeStruct((t, D_MODEL), F32),
        compiler_params=_params(("parallel",), 40),
        name="ple",
    )(x, p, g, w_gate, w_proj, g_final)


def _split_hi_lo(w):
    hi = w.astype(BF16)
    lo = (w - hi.astype(F32)).astype(BF16)
    return hi, lo


def _pack_params(g_mix, w_in, b_in, w_conv_dw, b_conv_dw, g_conv_ln, b_conv_ln, w_conv_out,
                 w_alpha_f, b_alpha_f, w_alpha_b, b_alpha_b, g_gla, w_gla_out, w_out,
                 g_ffn, w_ffn_up, w_ffn_dw, b_ffn_dw, w_ffn_down, g_ple, w_ple_gate, w_ple_proj):
    nl = g_mix.shape[0]
    rows = lambda v: v.reshape(nl, 1, -1).astype(F32)
    z0 = COL_SCONV
    z1 = z0 + 2 * GATE_RANK
    qs = GLA_DK ** -0.5
    w_main = jnp.concatenate(
        [w_in[:, :, :COL_Q], w_in[:, :, COL_Q:COL_K] * qs, w_in[:, :, COL_K:z0], w_in[:, :, z1:]], axis=2).astype(BF16)
    b_main = jnp.concatenate(
        [b_in[:, :COL_Q], b_in[:, COL_Q:COL_K] * qs, b_in[:, COL_K:z0], b_in[:, z1:]], axis=1).reshape(nl, 1, -1)
    wzf, wzb = w_in[:, :, z0:z0 + GATE_RANK], w_in[:, :, z0 + GATE_RANK:z1]
    used = 6 * GATE_RANK
    pad_w = jnp.zeros((nl, D_MODEL, LANES - used), F32)
    wz = jnp.concatenate([wzf, wzf, wzf, wzb, wzb, wzb, pad_w], axis=2).astype(BF16)
    bzf, bzb = b_in[:, z0:z0 + GATE_RANK], b_in[:, z0 + GATE_RANK:z1]
    bz = jnp.concatenate([bzf, bzf, bzf, bzb, bzb, bzb, jnp.ones((nl, 2), F32),
                          jnp.zeros((nl, LANES - used - 2), F32)], axis=1).reshape(nl, 1, -1)
    fhi, flo = _split_hi_lo(w_alpha_f)
    bhi, blo = _split_hi_lo(w_alpha_b)
    bias_f = jnp.stack(_split_hi_lo(b_alpha_f.astype(F32)), axis=1)
    bias_b = jnp.stack(_split_hi_lo(b_alpha_b.astype(F32)), axis=1)
    zrows = lambda n: jnp.zeros((nl, n, GLA_KDIM), BF16)
    wa_f = jnp.concatenate([fhi, fhi, flo, zrows(3 * GATE_RANK), bias_f, zrows(LANES - used - 2)], axis=1)
    wa_b = jnp.concatenate([zrows(3 * GATE_RANK), bhi, bhi, blo, bias_b, zrows(LANES - used - 2)], axis=1)
    wdw = jnp.concatenate([w_conv_dw, jnp.zeros((nl, 1, D_MODEL), F32)], axis=1)
    fdw = jnp.concatenate([w_ffn_dw, jnp.zeros((nl, 8 - w_ffn_dw.shape[1], FFN_DIM), F32)], axis=1)
    return dict(
        g_mix=rows(g_mix), w_main=w_main, b_main=b_main, wz=wz, bz=bz,
        wdw=wdw, bdw=rows(b_conv_dw), gln=rows(g_conv_ln), bln=rows(b_conv_ln),
        w_conv_out=w_conv_out.astype(BF16),
        wa_f=wa_f, wa_b=wa_b,
        g_gla=rows(g_gla), w_gla_out=w_gla_out.astype(BF16), w_out=w_out.astype(BF16),
        g_ffn=rows(g_ffn), w_ffn_up=w_ffn_up.astype(BF16), fdw=fdw, fdb=rows(b_ffn_dw),
        w_ffn_down=w_ffn_down.astype(BF16),
        g_ple=rows(g_ple), w_ple_gate=w_ple_gate.astype(BF16), w_ple_proj=w_ple_proj.astype(BF16),
    )


def _trunk(x, p, w, g_final):
    bsz, seq, _ = x.shape
    t = bsz * seq
    nl = p.shape[0]
    x = x.reshape(t, D_MODEL)
    p = p.reshape(nl, t, PLE_DIM)
    chunk = min(GLA_CHUNK, _tile(seq, GLA_ROWS))
    tri_f, tri_b = _tri(chunk, False), _tri(chunk, True)
    gf = g_final.reshape(1, -1)
    for li in range(nl):
        proj, zc = _inproj(x, li, w["g_mix"], w["w_main"], w["b_main"], w["wz"], w["bz"])
        o, cv_a = _gla_direction(proj, zc, li, w["wa_f"], tri_f, w["wdw"], w["bdw"], None, seq, False)
        o, cv_b = _gla_direction(proj, zc, li, w["wa_b"], tri_b, w["wdw"], w["bdw"], o, seq, True)
        x = _merge(o, proj, cv_a, cv_b, x, li, w["g_gla"], w["gln"], w["bln"],
                   w["w_conv_out"], w["w_gla_out"], w["w_out"])
        x = _ffn(x, seq, li, w["g_ffn"], w["w_ffn_up"], w["fdw"], w["fdb"], w["w_ffn_down"])
        x = _ple(x, p, li, w["g_ple"], w["w_ple_gate"], w["w_ple_proj"], gf, li == nl - 1)
    return x.reshape(bsz, seq, D_MODEL)


def kernel(x_prompt, x_sample, p_prompt, p_sample, g_mix, w_in, b_in, w_conv_dw, b_conv_dw, g_conv_ln,
           b_conv_ln, w_conv_out, w_alpha_f, b_alpha_f, w_alpha_b, b_alpha_b, g_gla, w_gla_out, w_out,
           g_ffn, w_ffn_up, w_ffn_dw, b_ffn_dw, w_ffn_down, g_ple, w_ple_gate, w_ple_proj, g_final):
    params = _pack_params(g_mix, w_in, b_in, w_conv_dw, b_conv_dw, g_conv_ln, b_conv_ln, w_conv_out,
                          w_alpha_f, b_alpha_f, w_alpha_b, b_alpha_b, g_gla, w_gla_out, w_out,
                          g_ffn, w_ffn_up, w_ffn_dw, b_ffn_dw, w_ffn_down, g_ple, w_ple_gate, w_ple_proj)
    y_prompt = _trunk(x_prompt, p_prompt, params, g_final)
    y_sample = _trunk(x_sample, p_sample, params, g_final)
    return (y_prompt, y_sample)
```
